```python
import jax, jax.numpy as jnp
from jax import lax
import numpy as np

D_MODEL = 1024
BATCH = 32
SEQ = 2048
DEPTH = 4

D_FF = 2816
LRU_WIDTH = D_MODEL
LRU_HEADS = 4
LRU_BLOCK = LRU_WIDTH // LRU_HEADS
LRU_CONV = 4
LRU_C = 8.0
CM_WIDTH = D_MODEL
CM_KERNEL = 31
GLA_HEADS = 4
GLA_DK = D_MODEL // 2 // GLA_HEADS
GLA_DV = D_MODEL // GLA_HEADS
GLA_RANK = 16
GLA_TAU = 16.0
GLA_CHUNK = 64
N_BRANCH = 3
EPS = 1e-6

IN_WIDTHS = (LRU_WIDTH,
             2 * CM_WIDTH,
             GLA_HEADS * GLA_DK,
             GLA_HEADS * GLA_DK,
             GLA_HEADS * GLA_DV,
             GLA_HEADS * GLA_DV,
             GLA_RANK,
             N_BRANCH * D_MODEL)
IN_COLS = sum(IN_WIDTHS)

kernel_name = "hybrid_rglru_conformer_gla_macaron"


def rmsnorm(x, g):
    xf = x.astype(jnp.float32)
    y = xf * lax.rsqrt(jnp.mean(xf * xf, axis=-1, keepdims=True) + EPS)
    return (y * g.astype(jnp.float32)).astype(x.dtype)


def layernorm(x, g, b):
    xf = x.astype(jnp.float32)
    mu = jnp.mean(xf, axis=-1, keepdims=True)
    xc = xf - mu
    y = xc * lax.rsqrt(jnp.mean(xc * xc, axis=-1, keepdims=True) + EPS)
    return (y * g.astype(jnp.float32) + b.astype(jnp.float32)).astype(x.dtype)


def swiglu(x, w_gate, w_up, w_down):
    return (jax.nn.silu(x @ w_gate) * (x @ w_up)) @ w_down


def causal_depthwise_conv(x, w, b):
    k = w.shape[0]
    y = lax.conv_general_dilated(x, w[:, None, :], window_strides=(1,), padding=[(k - 1, 0)],
                                 dimension_numbers=('NWC', 'WIO', 'NWC'),
                                 feature_group_count=x.shape[-1])
    return y + b


def rg_lru(u, w_a, b_a, w_x, b_x, lam):
    bsz, t, c = u.shape
    ub = u.reshape(bsz, t, LRU_HEADS, LRU_BLOCK)
    r = jax.nn.sigmoid(jnp.einsum('bthi,hij->bthj', ub, w_a).reshape(bsz, t, c) + b_a)
    i = jax.nn.sigmoid(jnp.einsum('bthi,hij->bthj', ub, w_x).reshape(bsz, t, c) + b_x)
    log_a = -LRU_C * r.astype(jnp.float32) * jax.nn.softplus(-lam.astype(jnp.float32))
    a = jnp.exp(log_a)
    mult = jnp.sqrt(-jnp.expm1(2.0 * log_a))
    bseq = mult * (i * u).astype(jnp.float32)

    def combine(left, right):
        al, bl = left
        ar, br = right
        return al * ar, ar * bl + br

    _, h = lax.associative_scan(combine, (a, bseq), axis=1)
    return h.astype(u.dtype)


def conformer_conv(g, conv_w, conv_b, ln_g, ln_b):
    val, gate = jnp.split(g, 2, axis=-1)
    u = val * jax.nn.sigmoid(gate)
    u = causal_depthwise_conv(u, conv_w, conv_b)
    u = layernorm(u, ln_g, ln_b)
    return jax.nn.silu(u)


def gla(q, k, v, g_low, w_g2, b_g):
    bsz, t, _ = q.shape
    n, l = t // GLA_CHUNK, GLA_CHUNK
    f32 = jnp.float32
    log_alpha = jax.nn.log_sigmoid(g_low.astype(f32) @ w_g2.astype(f32) + b_g.astype(f32)) / GLA_TAU
    qc = q.astype(f32).reshape(bsz, n, l, GLA_HEADS, GLA_DK) * (GLA_DK ** -0.5)
    kc = k.astype(f32).reshape(bsz, n, l, GLA_HEADS, GLA_DK)
    vc = v.astype(f32).reshape(bsz, n, l, GLA_HEADS, GLA_DV)
    bcum = jnp.cumsum(log_alpha.reshape(bsz, n, l, GLA_HEADS, GLA_DK), axis=2)
    b_last = bcum[:, :, -1]
    q_dec = qc * jnp.exp(bcum)
    k_intra = kc * jnp.exp(-bcum)
    k_state = kc * jnp.exp(b_last[:, :, None] - bcum)
    scores = jnp.einsum('bnihd,bnjhd->bnhij', q_dec, k_intra)
    mask = jnp.tril(jnp.ones((l, l), dtype=bool))
    scores = jnp.where(mask, scores, 0.0)
    intra = jnp.einsum('bnhij,bnjhe->bnihe', scores, vc)

    def step(s, xs):
        qd, ks, vv, dl = xs
        o = jnp.einsum('blhd,bhde->blhe', qd, s)
        s = s * jnp.exp(dl)[..., None] + jnp.einsum('blhd,blhe->bhde', ks, vv)
        return s, o

    s0 = jnp.zeros((bsz, GLA_HEADS, GLA_DK, GLA_DV), f32)
    xs = (jnp.moveaxis(q_dec, 1, 0), jnp.moveaxis(k_state, 1, 0),
          jnp.moveaxis(vc, 1, 0), jnp.moveaxis(b_last, 1, 0))
    _, inter = lax.scan(step, s0, xs)
    o = intra + jnp.moveaxis(inter, 0, 1)
    return o.reshape(bsz, t, GLA_HEADS, GLA_DV).astype(v.dtype)


def setup_inputs(seed: int = 0) -> dict:
    key = jax.random.key(seed)
    ks = iter(jax.random.split(key, 64))

    def nrm(shape, scale):
        return jax.random.normal(next(ks), shape, jnp.float32) * scale

    def gain(shape):
        return 1.0 + nrm(shape, 0.02)

    L, D, F = DEPTH, D_MODEL, D_FF
    a0 = jax.random.uniform(next(ks), (L, LRU_WIDTH), jnp.float32, 0.9, 0.999) ** (1.0 / LRU_C)
    lru_lambda = jnp.log(a0) - jnp.log1p(-a0)
    return {
        "x": nrm((BATCH, SEQ, D), 1.0),
        "ffn1_norm": gain((L, D)),
        "ffn1_w_gate": nrm((L, D, F), D ** -0.5),
        "ffn1_w_up": nrm((L, D, F), D ** -0.5),
        "ffn1_w_down": nrm((L, F, D), F ** -0.5),
        "mix_norm": gain((L, D)),
        "w_in": nrm((L, D, IN_COLS), D ** -0.5),
        "lru_conv_w": nrm((L, LRU_CONV, LRU_WIDTH), LRU_CONV ** -0.5),
        "lru_conv_b": nrm((L, LRU_WIDTH), 0.01),
        "lru_w_a": nrm((L, LRU_HEADS, LRU_BLOCK, LRU_BLOCK), LRU_BLOCK ** -0.5),
        "lru_b_a": nrm((L, LRU_WIDTH), 0.01),
        "lru_w_x": nrm((L, LRU_HEADS, LRU_BLOCK, LRU_BLOCK), LRU_BLOCK ** -0.5),
        "lru_b_x": nrm((L, LRU_WIDTH), 0.01),
        "lru_lambda": lru_lambda,
        "lru_w_o": nrm((L, LRU_WIDTH, D), LRU_WIDTH ** -0.5),
        "cm_conv_w": nrm((L, CM_KERNEL, CM_WIDTH), CM_KERNEL ** -0.5),
        "cm_conv_b": nrm((L, CM_WIDTH), 0.01),
        "cm_ln_g": gain((L, CM_WIDTH)),
        "cm_ln_b": nrm((L, CM_WIDTH), 0.01),
        "cm_w_o": nrm((L, CM_WIDTH, D), CM_WIDTH ** -0.5),
        "gla_w_g2": nrm((L, GLA_RANK, GLA_HEADS * GLA_DK), GLA_RANK ** -0.5),
        "gla_b_g": nrm((L, GLA_HEADS * GLA_DK), 0.1),
        "gla_norm": gain((L, GLA_DV)),
        "gla_w_o": nrm((L, GLA_HEADS * GLA_DV, D), (GLA_HEADS * GLA_DV) ** -0.5),
        "gate_b": nrm((L, N_BRANCH * D), 0.01),
        "w_out": nrm((L, D, D), D ** -0.5),
        "ffn2_norm": gain((L, D)),
        "ffn2_w_gate": nrm((L, D, F), D ** -0.5),
        "ffn2_w_up": nrm((L, D, F), D ** -0.5),
        "ffn2_w_down": nrm((L, F, D), F ** -0.5),
        "final_norm": gain((D,)),
    }


def reference(x, ffn1_norm, ffn1_w_gate, ffn1_w_up, ffn1_w_down, mix_norm, w_in,
              lru_conv_w, lru_conv_b, lru_w_a, lru_b_a, lru_w_x, lru_b_x, lru_lambda, lru_w_o,
              cm_conv_w, cm_conv_b, cm_ln_g, cm_ln_b, cm_w_o,
              gla_w_g2, gla_b_g, gla_norm, gla_w_o, gate_b, w_out,
              ffn2_norm, ffn2_w_gate, ffn2_w_up, ffn2_w_down, final_norm):
    bsz, t, d = x.shape
    split_idx = []
    acc = 0
    for w in IN_WIDTHS[:-1]:
        acc += w
        split_idx.append(acc)

    for i in range(DEPTH):
        x = x + 0.5 * swiglu(rmsnorm(x, ffn1_norm[i]), ffn1_w_gate[i], ffn1_w_up[i], ffn1_w_down[i])

        h = rmsnorm(x, mix_norm[i])
        proj = h @ w_in[i]
        lru_u, cm_g, q, k, v, og, g_low, mg = jnp.split(proj, split_idx, axis=-1)

        u = causal_depthwise_conv(lru_u, lru_conv_w[i], lru_conv_b[i])
        y_a = rg_lru(u, lru_w_a[i], lru_b_a[i], lru_w_x[i], lru_b_x[i], lru_lambda[i]) @ lru_w_o[i]

        y_b = conformer_conv(cm_g, cm_conv_w[i], cm_conv_b[i], cm_ln_g[i], cm_ln_b[i]) @ cm_w_o[i]

        o = rmsnorm(gla(q, k, v, g_low, gla_w_g2[i], gla_b_g[i]), gla_norm[i])
        y_c = (o.reshape(bsz, t, GLA_HEADS * GLA_DV) * jax.nn.silu(og)) @ gla_w_o[i]

        gates = jax.nn.sigmoid(mg + gate_b[i]).reshape(bsz, t, N_BRANCH, d)
        merged = gates[:, :, 0] * y_a + gates[:, :, 1] * y_b + gates[:, :, 2] * y_c
        x = x + merged @ w_out[i]

        x = x + 0.5 * swiglu(rmsnorm(x, ffn2_norm[i]), ffn2_w_gate[i], ffn2_w_up[i], ffn2_w_down[i])

    return rmsnorm(x, final_norm)
```

```python
import functools

import jax
import jax.numpy as jnp
from jax import lax
from jax.experimental import pallas as pl
from jax.experimental.pallas import tpu as pltpu

F32 = jnp.float32
BF16 = jnp.bfloat16

EPS = 1e-6
LRU_HEADS = 4
LRU_C = 8.0
GLA_HEADS = 4
GLA_TAU = 16.0
GLA_CHUNK = 64

LANES = 128
SUBLANES = 8
VMEM_LIMIT_BYTES = 56 * 1024 * 1024

FFN_ROWS = 512
FFN_COLS = 256
MIX_ROWS = 256


def _dot(a, b):
    return jnp.dot(a, b, preferred_element_type=F32)


def _sigmoid(x):
    return jax.nn.sigmoid(x)


def _rms_scale(x, g):
    ms = jnp.mean(x * x, axis=-1, keepdims=True)
    return x * lax.rsqrt(ms + EPS) * g


def _ffn_body(x_ref, g_ref, wg_ref, wu_ref, wd_ref, fin_ref, o_ref, act_ref, *, final):
    x = x_ref[...]
    h = _rms_scale(x, g_ref[...]).astype(BF16)
    d_ff = wg_ref.shape[1]
    for c in range(d_ff // FFN_COLS):
        sl = slice(c * FFN_COLS, (c + 1) * FFN_COLS)
        gate = _dot(h, wg_ref[:, sl])
        up = _dot(h, wu_ref[:, sl])
        act_ref[:, sl] = (gate * _sigmoid(gate) * up).astype(BF16)
    y = x + 0.5 * _dot(act_ref[...], wd_ref[...])
    if final:
        y = _rms_scale(y, fin_ref[...])
    o_ref[...] = y


def _ffn(x2, g, wg, wu, wd, fin, *, final):
    n, d = x2.shape
    d_ff = wg.shape[1]
    assert n % FFN_ROWS == 0 and d_ff % FFN_COLS == 0
    full = lambda a: pl.BlockSpec(a.shape, lambda i: (0,) * a.ndim)
    return pl.pallas_call(
        functools.partial(_ffn_body, final=final),
        grid=(n // FFN_ROWS,),
        in_specs=[pl.BlockSpec((FFN_ROWS, d), lambda i: (i, 0)),
                  full(g), full(wg), full(wu), full(wd), full(fin)],
        out_specs=pl.BlockSpec((FFN_ROWS, d), lambda i: (i, 0)),
        out_shape=jax.ShapeDtypeStruct((n, d), F32),
        scratch_shapes=[pltpu.VMEM((FFN_ROWS, d_ff), BF16)],
        compiler_params=pltpu.CompilerParams(
            dimension_semantics=("arbitrary",), vmem_limit_bytes=VMEM_LIMIT_BYTES),
        name="ffn_final" if final else "ffn",
    )(x2, g, wg, wu, wd, fin)


def _mixer_body(x_ref, mixn_ref, w_lru_ref, w_cm_ref, w_q_ref, w_k_ref, w_v_ref, w_og_ref,
                w_gl_ref, w_mg_ref,
                lcw_ref, lcb_ref, wa_ref, ba_ref, wx_ref, bx_ref, lam_ref, lwo_ref,
                ccw_ref, ccb_ref, lng_ref, lnb_ref, cwo_ref,
                wg2_ref, bg_ref, gn_ref, gwo_ref, gb_ref, wout_ref,
                o_ref,
                ubuf_ref, cbuf_ref, a_ref, b_ref, hst_ref, s_ref):
    rows, d = x_ref.shape
    lru_k = lcw_ref.shape[0]
    cm_k = ccw_ref.shape[0]
    u_pad = ubuf_ref.shape[0] - rows
    c_pad = cbuf_ref.shape[0] - rows

    @pl.when(pl.program_id(1) == 0)
    def _():
        ubuf_ref[0:u_pad, :] = jnp.zeros((u_pad, ubuf_ref.shape[1]), F32)
        cbuf_ref[0:c_pad, :] = jnp.zeros((c_pad, cbuf_ref.shape[1]), F32)
        hst_ref[...] = jnp.zeros(hst_ref.shape, F32)
        s_ref[...] = jnp.zeros(s_ref.shape, F32)

    x = x_ref[...]
    hb = _rms_scale(x, mixn_ref[...]).astype(BF16)

    ubuf_ref[u_pad:u_pad + rows, :] = _dot(hb, w_lru_ref[...])
    u = lcb_ref[...]
    for k in range(lru_k):
        off = u_pad - (lru_k - 1) + k
        u = u + lcw_ref[k:k + 1, :] * ubuf_ref[off:off + rows, :]
    ubuf_ref[0:u_pad, :] = ubuf_ref[rows:rows + u_pad, :]
    ub = u.astype(BF16)
    lam = lam_ref[...]
    cvec = -LRU_C * (jnp.maximum(-lam, 0.0) + jnp.log1p(jnp.exp(-jnp.abs(lam))))
    blk = d // LRU_HEADS
    for h in range(LRU_HEADS):
        sl = slice(h * blk, (h + 1) * blk)
        r = _sigmoid(_dot(ub[:, sl], wa_ref[h]) + ba_ref[:, sl])
        i = _sigmoid(_dot(ub[:, sl], wx_ref[h]) + bx_ref[:, sl])
        a = jnp.exp(r * cvec[:, sl])
        a_ref[:, sl] = a
        b_ref[:, sl] = jnp.sqrt(1.0 - a * a) * (i * u[:, sl])

    row8 = lax.broadcasted_iota(jnp.int32, (SUBLANES, d), 0)

    def scan_block(n, hprev):
        r0 = pl.multiple_of(n * SUBLANES, SUBLANES)
        a = a_ref[pl.ds(r0, SUBLANES), :]
        b = b_ref[pl.ds(r0, SUBLANES), :]
        for k in (1, 2, 4):
            keep = row8 >= k
            b = jnp.where(keep, a * pltpu.roll(b, k, 0) + b, b)
            a = jnp.where(keep, a * pltpu.roll(a, k, 0), a)
        hcur = a * hprev + b
        b_ref[pl.ds(r0, SUBLANES), :] = hcur
        return jnp.broadcast_to(hcur[SUBLANES - 1:SUBLANES, :], (SUBLANES, d))

    hst_ref[...] = lax.fori_loop(0, rows // SUBLANES, scan_block, hst_ref[...])
    y_a = _dot(b_ref[...].astype(BF16), lwo_ref[...])

    val = _dot(hb, w_cm_ref[:, 0:d])
    gate = _dot(hb, w_cm_ref[:, d:2 * d])
    cbuf_ref[c_pad:c_pad + rows, :] = val * _sigmoid(gate)
    acc = ccb_ref[...]
    for k in range(cm_k):
        off = c_pad - (cm_k - 1) + k
        acc = acc + ccw_ref[k:k + 1, :] * cbuf_ref[off:off + rows, :]
    cbuf_ref[0:c_pad, :] = cbuf_ref[rows:rows + c_pad, :]
    mu = jnp.mean(acc, axis=-1, keepdims=True)
    xc = acc - mu
    var = jnp.mean(xc * xc, axis=-1, keepdims=True)
    ln = xc * lax.rsqrt(var + EPS) * lng_ref[...] + lnb_ref[...]
    y_b = _dot((ln * _sigmoid(ln)).astype(BF16), cwo_ref[...])

    dk = w_q_ref.shape[1] // GLA_HEADS
    dv = w_v_ref.shape[1] // GLA_HEADS
    q = _dot(hb, w_q_ref[...])
    kk = _dot(hb, w_k_ref[...])
    v = _dot(hb, w_v_ref[...]).astype(BF16)
    og = _dot(hb, w_og_ref[...])
    glow = _dot(hb, w_gl_ref[...]).astype(BF16)
    z = _dot(glow, wg2_ref[...]) + bg_ref[...]
    la = (jnp.minimum(z, 0.0) - jnp.log1p(jnp.exp(-jnp.abs(z)))) * (1.0 / GLA_TAU)
    ri = lax.broadcasted_iota(jnp.int32, (rows, rows), 0)
    ci = lax.broadcasted_iota(jnp.int32, (rows, rows), 1)
    causal = (ri >= ci) & ((ri // GLA_CHUNK) == (ci // GLA_CHUNK))
    tri = jnp.where(causal, 1.0, 0.0).astype(BF16)
    la_hi = la.astype(BF16)
    la_lo = (la - la_hi.astype(F32)).astype(BF16)
    bcum = _dot(tri, la_hi) + _dot(tri, la_lo)
    n_chunks = rows // GLA_CHUNK
    blast = [bcum[(c + 1) * GLA_CHUNK - 1:(c + 1) * GLA_CHUNK, :] for c in range(n_chunks)]
    blast_full = jnp.concatenate(
        [jnp.broadcast_to(bl, (GLA_CHUNK, bl.shape[1])) for bl in blast], axis=0)
    q_dec = (q * (dk ** -0.5) * jnp.exp(bcum)).astype(BF16)
    k_intra = (kk * jnp.exp(-bcum)).astype(BF16)
    k_state = (kk * jnp.exp(blast_full - bcum)).astype(BF16)

    heads = []
    for h in range(GLA_HEADS):
        ks = slice(h * dk, (h + 1) * dk)
        vs = slice(h * dv, (h + 1) * dv)
        sc = lax.dot_general(q_dec[:, ks], k_intra[:, ks], (((1,), (1,)), ((), ())),
                             preferred_element_type=F32)
        o_h = _dot(jnp.where(causal, sc, 0.0).astype(BF16), v[:, vs])
        st = s_ref[h]
        inter = []
        for c in range(n_chunks):
            rs = slice(c * GLA_CHUNK, (c + 1) * GLA_CHUNK)
            inter.append(lax.dot_general(q_dec[rs, ks], st.astype(BF16),
                                         (((1,), (1,)), ((), ())), preferred_element_type=F32))
            st = st * jnp.exp(blast[c][:, ks]) + lax.dot_general(
                v[rs, vs], k_state[rs, ks], (((0,), (0,)), ((), ())),
                preferred_element_type=F32)
        s_ref[h] = st
        o_h = o_h + jnp.concatenate(inter, axis=0)
        o_h = _rms_scale(o_h, gn_ref[...])
        og_h = og[:, vs]
        heads.append((o_h * (og_h * _sigmoid(og_h))).astype(BF16))
    y_c = _dot(jnp.concatenate(heads, axis=1), gwo_ref[...])

    gates = _sigmoid(_dot(hb, w_mg_ref[...]) + gb_ref[...])
    merged = gates[:, 0:d] * y_a + gates[:, d:2 * d] * y_b + gates[:, 2 * d:3 * d] * y_c
    o_ref[...] = x + _dot(merged.astype(BF16), wout_ref[...])


_MIXER_PARAMS = [
    "mixn", "w_lru", "w_cm", "w_q", "w_k", "w_v", "w_og", "w_gl", "w_mg",
    "lcw", "lcb", "wa", "ba", "wx", "bx", "lam", "lwo",
    "ccw", "ccb", "lng", "lnb", "cwo",
    "wg2", "bg", "gn", "gwo", "gb", "wout",
]


def _mixer(x2, seq_len, weights):
    n, d = x2.shape
    assert seq_len % MIX_ROWS == 0 and MIX_ROWS % GLA_CHUNK == 0
    tiles = seq_len // MIX_ROWS
    batch = n // seq_len
    full = lambda a: pl.BlockSpec(a.shape, lambda b, j: (0,) * a.ndim)
    row_spec = pl.BlockSpec((MIX_ROWS, d), lambda b, j: (b * tiles + j, 0))
    names = list(weights)
    weights = [weights[k] for k in names]
    assert names == _MIXER_PARAMS
    lru_k = weights[names.index("lcw")].shape[0]
    cm_k = weights[names.index("ccw")].shape[0]
    u_pad = -(-(lru_k - 1) // SUBLANES) * SUBLANES
    c_pad = -(-(cm_k - 1) // SUBLANES) * SUBLANES
    dk = weights[names.index("w_q")].shape[1] // GLA_HEADS
    dv = weights[names.index("w_v")].shape[1] // GLA_HEADS
    return pl.pallas_call(
        _mixer_body,
        grid=(batch, tiles),
        in_specs=[row_spec] + [full(w) for w in weights],
        out_specs=row_spec,
        out_shape=jax.ShapeDtypeStruct((n, d), F32),
        scratch_shapes=[
            pltpu.VMEM((MIX_ROWS + u_pad, d), F32),
            pltpu.VMEM((MIX_ROWS + c_pad, d), F32),
            pltpu.VMEM((MIX_ROWS, d), F32),
            pltpu.VMEM((MIX_ROWS, d), F32),
            pltpu.VMEM((SUBLANES, d), F32),
            pltpu.VMEM((GLA_HEADS, dv, dk), F32),
        ],
        compiler_params=pltpu.CompilerParams(
            dimension_semantics=("arbitrary", "arbitrary"), vmem_limit_bytes=VMEM_LIMIT_BYTES),
        name="mixer",
    )(x2, *weights)


def _row(v):
    return v.reshape(1, -1).astype(F32)


def kernel(x, ffn1_norm, ffn1_w_gate, ffn1_w_up, ffn1_w_down, mix_norm, w_in, lru_conv_w, lru_conv_b, lru_w_a, lru_b_a, lru_w_x, lru_b_x, lru_lambda, lru_w_o, cm_conv_w, cm_conv_b, cm_ln_g, cm_ln_b, cm_w_o, gla_w_g2, gla_b_g, gla_norm, gla_w_o, gate_b, w_out, ffn2_norm, ffn2_w_gate, ffn2_w_up, ffn2_w_down, final_norm):
    bsz, t, d = x.shape
    depth = w_in.shape[0]
    lru_w = lru_w_o.shape[1]
    cm_w = cm_w_o.shape[1]
    qk_w = gla_w_g2.shape[2]
    v_w = gla_w_o.shape[1]
    rank = gla_w_g2.shape[1]
    widths = (lru_w, 2 * cm_w, qk_w, qk_w, v_w, v_w, rank, gate_b.shape[1])
    offs = [0]
    for w in widths:
        offs.append(offs[-1] + w)
    assert offs[-1] == w_in.shape[2]
    rank_pad = -(-rank // LANES) * LANES

    x2 = x.reshape(bsz * t, d)
    fin = _row(final_norm)
    for i in range(depth):
        x2 = _ffn(x2, _row(ffn1_norm[i]), ffn1_w_gate[i].astype(BF16), ffn1_w_up[i].astype(BF16),
                  ffn1_w_down[i].astype(BF16), fin, final=False)
        wi = w_in[i].astype(BF16)
        seg = [wi[:, offs[s]:offs[s + 1]] for s in range(len(widths))]
        w_gl = jnp.pad(seg[6], ((0, 0), (0, rank_pad - rank)))
        w_g2 = jnp.pad(gla_w_g2[i].astype(BF16), ((0, rank_pad - rank), (0, 0)))
        weights = dict(
            mixn=_row(mix_norm[i]), w_lru=seg[0], w_cm=seg[1], w_q=seg[2], w_k=seg[3], w_v=seg[4],
            w_og=seg[5], w_gl=w_gl, w_mg=seg[7],
            lcw=lru_conv_w[i], lcb=_row(lru_conv_b[i]), wa=lru_w_a[i].astype(BF16),
            ba=_row(lru_b_a[i]), wx=lru_w_x[i].astype(BF16), bx=_row(lru_b_x[i]),
            lam=_row(lru_lambda[i]), lwo=lru_w_o[i].astype(BF16),
            ccw=cm_conv_w[i], ccb=_row(cm_conv_b[i]), lng=_row(cm_ln_g[i]), lnb=_row(cm_ln_b[i]),
            cwo=cm_w_o[i].astype(BF16),
            wg2=w_g2, bg=_row(gla_b_g[i]), gn=_row(gla_norm[i]), gwo=gla_w_o[i].astype(BF16),
            gb=_row(gate_b[i]), wout=w_out[i].astype(BF16),
        )
        x2 = _mixer(x2, t, weights)
        x2 = _ffn(x2, _row(ffn2_norm[i]), ffn2_w_gate[i].astype(BF16), ffn2_w_up[i].astype(BF16),
                  ffn2_w_down[i].astype(BF16), fin, final=(i == depth - 1))
    return x2.reshape(bsz, t, d)
```

```python
import functools

import jax
import jax.numpy as jnp
from jax import lax
from jax.experimental import pallas as pl
from jax.experimental.pallas import tpu as pltpu

F32 = jnp.float32
BF16 = jnp.bfloat16

EPS = 1e-6
LRU_HEADS = 4
LRU_C = 8.0
GLA_HEADS = 4
GLA_TAU = 16.0
GLA_CHUNK = 64

LANES = 128
SUBLANES = 8
VMEM_LIMIT_BYTES = 56 * 1024 * 1024

FFN_ROWS = 512
FFN_COLS = 256
MIX_ROWS = 256
SEG = MIX_ROWS // SUBLANES
CONV_RB = 8


def _dot(a, b):
    return jnp.dot(a, b, preferred_element_type=F32)


def _sigmoid(x):
    return jax.nn.sigmoid(x)


def _rms_scale(x, g):
    ms = jnp.mean(x * x, axis=-1, keepdims=True)
    return x * lax.rsqrt(ms + EPS) * g


def _ffn_body(x_ref, g_ref, wg_ref, wu_ref, wd_ref, fin_ref, o_ref, act_ref, *, final):
    x = x_ref[...]
    h = _rms_scale(x, g_ref[...]).astype(BF16)
    d_ff = wg_ref.shape[1]
    for c in range(d_ff // FFN_COLS):
        sl = slice(c * FFN_COLS, (c + 1) * FFN_COLS)
        gate = _dot(h, wg_ref[:, sl])
        up = _dot(h, wu_ref[:, sl])
        act_ref[:, sl] = (gate * _sigmoid(gate) * up).astype(BF16)
    y = x + 0.5 * _dot(act_ref[...], wd_ref[...])
    if final:
        y = _rms_scale(y, fin_ref[...])
    o_ref[...] = y


def _ffn(x2, g, wg, wu, wd, fin, *, final):
    n, d = x2.shape
    d_ff = wg.shape[1]
    assert n % FFN_ROWS == 0 and d_ff % FFN_COLS == 0
    full = lambda a: pl.BlockSpec(a.shape, lambda i: (0,) * a.ndim)
    return pl.pallas_call(
        functools.partial(_ffn_body, final=final),
        grid=(n // FFN_ROWS,),
        in_specs=[pl.BlockSpec((FFN_ROWS, d), lambda i: (i, 0)),
                  full(g), full(wg), full(wu), full(wd), full(fin)],
        out_specs=pl.BlockSpec((FFN_ROWS, d), lambda i: (i, 0)),
        out_shape=jax.ShapeDtypeStruct((n, d), F32),
        scratch_shapes=[pltpu.VMEM((FFN_ROWS, d_ff), BF16)],
        compiler_params=pltpu.CompilerParams(
            dimension_semantics=("arbitrary",), vmem_limit_bytes=VMEM_LIMIT_BYTES),
        name="ffn_final" if final else "ffn",
    )(x2, g, wg, wu, wd, fin)


def _load_interleaved(col_refs):
    return jnp.concatenate(
        [jnp.concatenate([ref[pl.ds(i, SUBLANES, stride=SEG), :] for ref in col_refs], axis=1)
         for i in range(SEG)], axis=0)


def _load_natural(ref):
    per = SEG // SUBLANES
    return jnp.concatenate(
        [jnp.concatenate(
            [ref[c, pl.ds(SUBLANES * SUBLANES * (r % per) + r // per, SUBLANES, stride=SUBLANES), :]
             for c in range(ref.shape[0])], axis=1)
         for r in range(SEG)], axis=0)


def _conv_stage(cur, taps, buf_ref, tail_ref):
    rows, c = cur.shape
    hist = SUBLANES * (taps - 1)
    buf_ref[hist:hist + rows, :] = cur
    new_tail = buf_ref[rows:rows + hist, :]
    sub = lax.broadcasted_iota(jnp.int32, (taps - 1, SUBLANES, c), 1)
    from_prev = pltpu.roll(tail_ref[...].reshape(taps - 1, SUBLANES, c), 1, 1)
    from_cur = pltpu.roll(new_tail.reshape(taps - 1, SUBLANES, c), 1, 1)
    buf_ref[0:hist, :] = jnp.where(sub == 0, from_prev, from_cur).reshape(hist, c)
    tail_ref[...] = new_tail


def _conv_weights_stage(w_ref, bias_ref, wb_ref):
    taps = w_ref.shape[0]
    for k in range(taps):
        wb_ref[SUBLANES * k:SUBLANES * (k + 1), :] = jnp.broadcast_to(
            w_ref[k:k + 1, :], (SUBLANES, w_ref.shape[1]))
    wb_ref[SUBLANES * taps:SUBLANES * (taps + 1), :] = jnp.broadcast_to(
        bias_ref[...], (SUBLANES, w_ref.shape[1]))


def _conv_block(wb_ref, buf_ref, out_ref, taps, lane_group, row_block):
    hist = SUBLANES * (taps - 1)
    cs = slice(LANES * lane_group, LANES * (lane_group + 1))
    i0 = row_block * CONV_RB
    blk = lambda ref, i: ref[SUBLANES * i:SUBLANES * (i + 1), cs]
    win = [blk(buf_ref, taps - 1 + i0 + r) for r in range(CONV_RB)]
    bias = blk(wb_ref, taps)
    acc = [bias] * CONV_RB
    for delay in range(taps):
        wv = blk(wb_ref, taps - 1 - delay)
        acc = [acc[r] + wv * win[r] for r in range(CONV_RB)]
        if delay + 1 < taps:
            win = [blk(buf_ref, taps - 1 + i0 - delay - 1)] + win[:-1]
    for r in range(CONV_RB):
        out_ref[SUBLANES * (i0 + r):SUBLANES * (i0 + r + 1), cs] = acc[r]


def _conv_blocks(wb_ref, buf_ref, out_ref, taps, lane_groups):
    for c in lane_groups:
        for rb in range(SEG // CONV_RB):
            _conv_block(wb_ref, buf_ref, out_ref, taps, c, rb)


def _sublane_affine_scan(a, b):
    row = lax.broadcasted_iota(jnp.int32, a.shape, 0)
    for k in (1, 2, 4):
        keep = row >= k
        b = jnp.where(keep, a * pltpu.roll(b, k, 0) + b, b)
        a = jnp.where(keep, a * pltpu.roll(a, k, 0), a)
    return a, b


def _mixer_body(*refs, n_col):
    _mixer_tile(refs[:n_col], *refs[n_col:])


def _mixer_tile(x_refs, mixn_ref, w_lru_ref, w_cm_ref, w_q_ref, w_k_ref, w_v_ref, w_og_ref,
                w_gl_ref, w_mgab_ref, w_mgc_ref,
                lcw_ref, lcb_ref, wa_ref, ba_ref, wx_ref, bx_ref, lam_ref, lwo_ref,
                ccw_ref, ccb_ref, lng_ref, lnb_ref, cwo_ref,
                wg2_ref, bg_ref, gn_ref, gwo_ref, gb_ref, wout_ref,
                o_ref,
                hb_ref, hbi_ref, ubuf_ref, utail_ref, uwb_ref, cbuf_ref, ctail_ref, cwb_ref, cout_ref,
                a_ref, b_ref, mab_ref, hst_ref, s_ref):
    rows, d = o_ref.shape

    @pl.when(pl.program_id(1) == 0)
    def _():
        utail_ref[...] = jnp.zeros(utail_ref.shape, F32)
        ctail_ref[...] = jnp.zeros(ctail_ref.shape, F32)
        hst_ref[...] = jnp.zeros(hst_ref.shape, F32)
        s_ref[...] = jnp.zeros(s_ref.shape, F32)
        _conv_weights_stage(lcw_ref, lcb_ref, uwb_ref)
        _conv_weights_stage(ccw_ref, ccb_ref, cwb_ref)

    x = jnp.concatenate([ref[...] for ref in x_refs], axis=1)
    hb_ref[...] = _rms_scale(x, mixn_ref[...]).astype(BF16)
    hbi_ref[...] = _rms_scale(_load_interleaved(x_refs), mixn_ref[...]).astype(BF16)

    dk = w_q_ref.shape[1] // GLA_HEADS
    dv = w_v_ref.shape[1] // GLA_HEADS
    lru_taps = lcw_ref.shape[0]
    cm_taps = ccw_ref.shape[0]
    lane_groups = d // LANES

    _conv_stage(_dot(hbi_ref[...], w_lru_ref[...]), lru_taps, ubuf_ref, utail_ref)
    val = _dot(hbi_ref[...], w_cm_ref[:, 0:d])
    _conv_blocks(uwb_ref, ubuf_ref, b_ref, lru_taps, range(lane_groups))
    gate = _dot(hbi_ref[...], w_cm_ref[:, d:2 * d])
    u = b_ref[...]
    ub = u.astype(BF16)
    lam = lam_ref[...]
    cvec = -LRU_C * (jnp.maximum(-lam, 0.0) + jnp.log1p(jnp.exp(-jnp.abs(lam))))
    blk = d // LRU_HEADS
    for h in range(LRU_HEADS):
        sl = slice(h * blk, (h + 1) * blk)
        r = _sigmoid(_dot(ub[:, sl], wa_ref[h]) + ba_ref[:, sl])
        i = _sigmoid(_dot(ub[:, sl], wx_ref[h]) + bx_ref[:, sl])
        a = jnp.exp(r * cvec[:, sl])
        a_ref[:, sl] = a
        b_ref[:, sl] = jnp.sqrt(1.0 - a * a) * (i * u[:, sl])
    _conv_stage(val * _sigmoid(gate), cm_taps, cbuf_ref, ctail_ref)
    q = _dot(hb_ref[...], w_q_ref[...])
    kk = _dot(hb_ref[...], w_k_ref[...])
    prod = lin = None
    for i in range(SEG):
        rs = slice(SUBLANES * i, SUBLANES * (i + 1))
        a = a_ref[rs, :]
        b = b_ref[rs, :]
        if i == 0:
            prod, lin = a, b
        else:
            lin = a * lin + b
            prod = a * prod
        a_ref[rs, :] = prod
        b_ref[rs, :] = lin
    h_prev = hst_ref[...]
    run_a, run_b = _sublane_affine_scan(prod, lin)
    run_end = run_a * h_prev + run_b
    row8 = lax.broadcasted_iota(jnp.int32, run_end.shape, 0)
    run_in = jnp.where(row8 == 0, h_prev, pltpu.roll(run_end, 1, 0))
    hst_ref[...] = jnp.broadcast_to(run_end[SUBLANES - 1:SUBLANES, :], run_end.shape)
    hseq = (b_ref[...] + a_ref[...] * jnp.tile(run_in, (SEG, 1))).astype(BF16)

    quarter = lane_groups // 4
    _conv_blocks(cwb_ref, cbuf_ref, cout_ref, cm_taps, range(0, quarter))
    v = _dot(hb_ref[...], w_v_ref[...]).astype(BF16)
    _conv_blocks(cwb_ref, cbuf_ref, cout_ref, cm_taps, range(quarter, 2 * quarter))
    og = _dot(hb_ref[...], w_og_ref[...])
    _conv_blocks(cwb_ref, cbuf_ref, cout_ref, cm_taps, range(2 * quarter, 3 * quarter))
    y_a = _dot(hseq, lwo_ref[...])
    _conv_blocks(cwb_ref, cbuf_ref, cout_ref, cm_taps, range(3 * quarter, lane_groups))
    g_ab = _dot(hbi_ref[...], w_mgab_ref[...])
    acc = cout_ref[...]
    mu = jnp.mean(acc, axis=-1, keepdims=True)
    xc = acc - mu
    var = jnp.mean(xc * xc, axis=-1, keepdims=True)
    ln = xc * lax.rsqrt(var + EPS) * lng_ref[...] + lnb_ref[...]
    glow = _dot(hb_ref[...], w_gl_ref[...]).astype(BF16)
    y_b = _dot((ln * _sigmoid(ln)).astype(BF16), cwo_ref[...])

    g_ab = _sigmoid(g_ab + gb_ref[:, 0:2 * d])
    merged_i = g_ab[:, 0:d] * y_a + g_ab[:, d:2 * d] * y_b
    for c in range(lane_groups):
        mab_ref[c] = merged_i[:, c * LANES:(c + 1) * LANES]
    merged_ab = _load_natural(mab_ref)

    z = _dot(glow, wg2_ref[...]) + bg_ref[...]
    la = (jnp.minimum(z, 0.0) - jnp.log1p(jnp.exp(-jnp.abs(z)))) * (1.0 / GLA_TAU)
    ri = lax.broadcasted_iota(jnp.int32, (rows, rows), 0)
    ci = lax.broadcasted_iota(jnp.int32, (rows, rows), 1)
    causal = (ri >= ci) & ((ri // GLA_CHUNK) == (ci // GLA_CHUNK))
    tri = jnp.where(causal, 1.0, 0.0).astype(BF16)
    la_hi = la.astype(BF16)
    la_lo = (la - la_hi.astype(F32)).astype(BF16)
    bcum = _dot(tri, la_hi) + _dot(tri, la_lo)
    n_chunks = rows // GLA_CHUNK
    blast = [bcum[(c + 1) * GLA_CHUNK - 1:(c + 1) * GLA_CHUNK, :] for c in range(n_chunks)]
    blast_full = jnp.concatenate(
        [jnp.broadcast_to(bl, (GLA_CHUNK, bl.shape[1])) for bl in blast], axis=0)
    q_dec = (q * (dk ** -0.5) * jnp.exp(bcum)).astype(BF16)
    k_intra = (kk * jnp.exp(-bcum)).astype(BF16)
    k_state = (kk * jnp.exp(blast_full - bcum)).astype(BF16)

    heads = []
    for h in range(GLA_HEADS):
        ks = slice(h * dk, (h + 1) * dk)
        vs = slice(h * dv, (h + 1) * dv)
        sc = lax.dot_general(q_dec[:, ks], k_intra[:, ks], (((1,), (1,)), ((), ())),
                             preferred_element_type=F32)
        o_h = _dot(jnp.where(causal, sc, 0.0).astype(BF16), v[:, vs])
        st = s_ref[h]
        inter = []
        for c in range(n_chunks):
            rs = slice(c * GLA_CHUNK, (c + 1) * GLA_CHUNK)
            inter.append(lax.dot_general(q_dec[rs, ks], st.astype(BF16),
                                         (((1,), (1,)), ((), ())), preferred_element_type=F32))
            st = st * jnp.exp(blast[c][:, ks]) + lax.dot_general(
                v[rs, vs], k_state[rs, ks], (((0,), (0,)), ((), ())),
                preferred_element_type=F32)
        s_ref[h] = st
        o_h = o_h + jnp.concatenate(inter, axis=0)
        o_h = _rms_scale(o_h, gn_ref[...])
        og_h = og[:, vs]
        heads.append((o_h * (og_h * _sigmoid(og_h))).astype(BF16))
    y_c = _dot(jnp.concatenate(heads, axis=1), gwo_ref[...])

    g_c = _sigmoid(_dot(hb_ref[...], w_mgc_ref[...]) + gb_ref[:, 2 * d:3 * d])
    merged = merged_ab + g_c * y_c
    o_ref[...] = x + _dot(merged.astype(BF16), wout_ref[...])


_MIXER_PARAMS = [
    "mixn", "w_lru", "w_cm", "w_q", "w_k", "w_v", "w_og", "w_gl", "w_mgab", "w_mgc",
    "lcw", "lcb", "wa", "ba", "wx", "bx", "lam", "lwo",
    "ccw", "ccb", "lng", "lnb", "cwo",
    "wg2", "bg", "gn", "gwo", "gb", "wout",
]


def _mixer(x2, seq_len, weights):
    n, d = x2.shape
    assert seq_len % MIX_ROWS == 0 and MIX_ROWS % GLA_CHUNK == 0 and SEG % SUBLANES == 0
    tiles = seq_len // MIX_ROWS
    batch = n // seq_len
    full = lambda a: pl.BlockSpec(a.shape, lambda b, j: (0,) * a.ndim)
    row_spec = pl.BlockSpec((MIX_ROWS, d), lambda b, j: (b * tiles + j, 0))
    n_col = d // LANES
    col_specs = [pl.BlockSpec((MIX_ROWS, LANES), lambda b, j, c=c: (b * tiles + j, c))
                 for c in range(n_col)]
    assert list(weights) == _MIXER_PARAMS
    u_hist = SUBLANES * (weights["lcw"].shape[0] - 1)
    c_hist = SUBLANES * (weights["ccw"].shape[0] - 1)
    assert max(weights["lcw"].shape[0], weights["ccw"].shape[0]) <= SEG
    dk = weights["w_q"].shape[1] // GLA_HEADS
    dv = weights["w_v"].shape[1] // GLA_HEADS
    return pl.pallas_call(
        functools.partial(_mixer_body, n_col=n_col),
        grid=(batch, tiles),
        in_specs=col_specs + [full(w) for w in weights.values()],
        out_specs=row_spec,
        out_shape=jax.ShapeDtypeStruct((n, d), F32),
        scratch_shapes=[
            pltpu.VMEM((MIX_ROWS, d), BF16),
            pltpu.VMEM((MIX_ROWS, d), BF16),
            pltpu.VMEM((MIX_ROWS + u_hist, d), F32),
            pltpu.VMEM((u_hist, d), F32),
            pltpu.VMEM((u_hist + 2 * SUBLANES, d), F32),
            pltpu.VMEM((MIX_ROWS + c_hist, d), F32),
            pltpu.VMEM((c_hist, d), F32),
            pltpu.VMEM((c_hist + 2 * SUBLANES, d), F32),
            pltpu.VMEM((MIX_ROWS, d), F32),
            pltpu.VMEM((MIX_ROWS, d), F32),
            pltpu.VMEM((MIX_ROWS, d), F32),
            pltpu.VMEM((n_col, MIX_ROWS, LANES), F32),
            pltpu.VMEM((SUBLANES, d), F32),
            pltpu.VMEM((GLA_HEADS, dv, dk), F32),
        ],
        compiler_params=pltpu.CompilerParams(
            dimension_semantics=("arbitrary", "arbitrary"), vmem_limit_bytes=VMEM_LIMIT_BYTES),
        name="mixer",
    )(*([x2] * n_col), *weights.values())


def _row(v):
    return v.reshape(1, -1).astype(F32)


def kernel(x, ffn1_norm, ffn1_w_gate, ffn1_w_up, ffn1_w_down, mix_norm, w_in, lru_conv_w, lru_conv_b, lru_w_a, lru_b_a, lru_w_x, lru_b_x, lru_lambda, lru_w_o, cm_conv_w, cm_conv_b, cm_ln_g, cm_ln_b, cm_w_o, gla_w_g2, gla_b_g, gla_norm, gla_w_o, gate_b, w_out, ffn2_norm, ffn2_w_gate, ffn2_w_up, ffn2_w_down, final_norm):
    bsz, t, d = x.shape
    depth = w_in.shape[0]
    lru_w = lru_w_o.shape[1]
    cm_w = cm_w_o.shape[1]
    qk_w = gla_w_g2.shape[2]
    v_w = gla_w_o.shape[1]
    rank = gla_w_g2.shape[1]
    widths = (lru_w, 2 * cm_w, qk_w, qk_w, v_w, v_w, rank, gate_b.shape[1])
    offs = [0]
    for w in widths:
        offs.append(offs[-1] + w)
    assert offs[-1] == w_in.shape[2]
    rank_pad = -(-rank // LANES) * LANES

    x2 = x.reshape(bsz * t, d)
    fin = _row(final_norm)
    for i in range(depth):
        x2 = _ffn(x2, _row(ffn1_norm[i]), ffn1_w_gate[i].astype(BF16), ffn1_w_up[i].astype(BF16),
                  ffn1_w_down[i].astype(BF16), fin, final=False)
        wi = w_in[i].astype(BF16)
        seg = [wi[:, offs[s]:offs[s + 1]] for s in range(len(widths))]
        w_gl = jnp.pad(seg[6], ((0, 0), (0, rank_pad - rank)))
        w_g2 = jnp.pad(gla_w_g2[i].astype(BF16), ((0, rank_pad - rank), (0, 0)))
        weights = dict(
            mixn=_row(mix_norm[i]), w_lru=seg[0], w_cm=seg[1], w_q=seg[2], w_k=seg[3], w_v=seg[4],
            w_og=seg[5], w_gl=w_gl, w_mgab=seg[7][:, 0:2 * d], w_mgc=seg[7][:, 2 * d:3 * d],
            lcw=lru_conv_w[i], lcb=_row(lru_conv_b[i]), wa=lru_w_a[i].astype(BF16),
            ba=_row(lru_b_a[i]), wx=lru_w_x[i].astype(BF16), bx=_row(lru_b_x[i]),
            lam=_row(lru_lambda[i]), lwo=lru_w_o[i].astype(BF16),
            ccw=cm_conv_w[i], ccb=_row(cm_conv_b[i]), lng=_row(cm_ln_g[i]), lnb=_row(cm_ln_b[i]),
            cwo=cm_w_o[i].astype(BF16),
            wg2=w_g2, bg=_row(gla_b_g[i]), gn=_row(gla_norm[i]), gwo=gla_w_o[i].astype(BF16),
            gb=_row(gate_b[i]), wout=w_out[i].astype(BF16),
        )
        x2 = _mixer(x2, t, weights)
        x2 = _ffn(x2, _row(ffn2_norm[i]), ffn2_w_gate[i].astype(BF16), ffn2_w_up[i].astype(BF16),
                  ffn2_w_down[i].astype(BF16), fin, final=(i == depth - 1))
    return x2.reshape(bsz, t, d)
```

```python
import functools

import jax
import jax.numpy as jnp
from jax import lax
from jax.experimental import pallas as pl
from jax.experimental.pallas import tpu as pltpu

F32 = jnp.float32
BF16 = jnp.bfloat16

EPS = 1e-6
LRU_HEADS = 4
LRU_C = 8.0
GLA_HEADS = 4
GLA_TAU = 16.0
GLA_CHUNK = 64

LANES = 128
SUBLANES = 8
MXU_COLS = 256
VMEM_LIMIT_BYTES = 56 * 1024 * 1024

FFN_ROWS = 512
MIX_ROWS = 256
SEG = MIX_ROWS // SUBLANES
SEG_PITCH = SEG + SUBLANES
CONV_RB = 8
PAIR_STEPS = 4


def _dot(a, b):
    return jnp.dot(a, b, preferred_element_type=F32)


def _sigmoid(x):
    return jax.nn.sigmoid(x)


def _rms_scale(x, g):
    ms = jnp.mean(x * x, axis=-1, keepdims=True)
    return x * lax.rsqrt(ms + EPS) * g


def _ffn_body(x_ref, g_ref, wg_ref, wu_ref, wd_ref, fin_ref, o_ref, act_ref, *, final):
    x = x_ref[...]
    h = _rms_scale(x, g_ref[...]).astype(BF16)
    d_ff = wg_ref.shape[1]
    for c in range(d_ff // MXU_COLS):
        sl = slice(c * MXU_COLS, (c + 1) * MXU_COLS)
        gate = _dot(h, wg_ref[:, sl])
        up = _dot(h, wu_ref[:, sl])
        act_ref[:, sl] = (gate * _sigmoid(gate) * up).astype(BF16)
    y = x + 0.5 * _dot(act_ref[...], wd_ref[...])
    if final:
        y = _rms_scale(y, fin_ref[...])
    o_ref[...] = y


def _ffn(x2, g, wg, wu, wd, fin, *, final):
    n, d = x2.shape
    d_ff = wg.shape[1]
    assert n % FFN_ROWS == 0 and d_ff % MXU_COLS == 0
    full = lambda a: pl.BlockSpec(a.shape, lambda i: (0,) * a.ndim)
    return pl.pallas_call(
        functools.partial(_ffn_body, final=final),
        grid=(n // FFN_ROWS,),
        in_specs=[pl.BlockSpec((FFN_ROWS, d), lambda i: (i, 0)),
                  full(g), full(wg), full(wu), full(wd), full(fin)],
        out_specs=pl.BlockSpec((FFN_ROWS, d), lambda i: (i, 0)),
        out_shape=jax.ShapeDtypeStruct((n, d), F32),
        scratch_shapes=[pltpu.VMEM((FFN_ROWS, d_ff), BF16)],
        compiler_params=pltpu.CompilerParams(
            dimension_semantics=("arbitrary",), vmem_limit_bytes=VMEM_LIMIT_BYTES),
        name="ffn_final" if final else "ffn",
    )(x2, g, wg, wu, wd, fin)


def _load_interleaved(col_refs, xs_ref):
    for c, ref in enumerate(col_refs):
        for j in range(SUBLANES):
            xs_ref[c, SEG_PITCH * j:SEG_PITCH * j + SEG, :] = ref[SEG * j:SEG * (j + 1), :]
    return jnp.concatenate(
        [jnp.concatenate([xs_ref[c, pl.ds(i, SUBLANES, stride=SEG_PITCH), :]
                          for c in range(len(col_refs))], axis=1)
         for i in range(SEG)], axis=0)


def _load_natural(ref):
    per = SEG // SUBLANES
    return jnp.concatenate(
        [jnp.concatenate(
            [ref[c, pl.ds(SUBLANES * SUBLANES * (r % per) + r // per, SUBLANES, stride=SUBLANES), :]
             for c in range(ref.shape[0])], axis=1)
         for r in range(SEG)], axis=0)


def _lane_concat(ref):
    return jnp.concatenate([ref[c] for c in range(ref.shape[0])], axis=1)


def _conv_stage(cur, taps, buf_ref, tail_ref):
    rows = cur.shape[0]
    hist = SUBLANES * (taps - 1)
    sub = lax.broadcasted_iota(jnp.int32, (taps - 1, SUBLANES, LANES), 1)
    for c in range(buf_ref.shape[0]):
        buf_ref[c, hist:hist + rows, :] = cur[:, LANES * c:LANES * (c + 1)]
        new_tail = buf_ref[c, rows:rows + hist, :]
        from_prev = pltpu.roll(tail_ref[c].reshape(taps - 1, SUBLANES, LANES), 1, 1)
        from_cur = pltpu.roll(new_tail.reshape(taps - 1, SUBLANES, LANES), 1, 1)
        buf_ref[c, 0:hist, :] = jnp.where(sub == 0, from_prev, from_cur).reshape(hist, LANES)
        tail_ref[c] = new_tail


def _conv_weights_stage(w_ref, bias_ref, wb_ref):
    taps = w_ref.shape[0]
    for c in range(wb_ref.shape[0]):
        cs = slice(LANES * c, LANES * (c + 1))
        for k in range(taps):
            wb_ref[c, SUBLANES * k:SUBLANES * (k + 1), :] = jnp.broadcast_to(
                w_ref[k:k + 1, cs], (SUBLANES, LANES))
        wb_ref[c, SUBLANES * taps:SUBLANES * (taps + 1), :] = jnp.broadcast_to(
            bias_ref[:, cs], (SUBLANES, LANES))


def _conv_block(wb_ref, buf_ref, out_ref, taps, lane_group, row_block):
    i0 = row_block * CONV_RB
    blk = lambda ref, i: ref[lane_group, SUBLANES * i:SUBLANES * (i + 1), :]
    win = [blk(buf_ref, taps - 1 + i0 + r) for r in range(CONV_RB)]
    bias = blk(wb_ref, taps)
    acc = [bias] * CONV_RB
    for delay in range(taps):
        wv = blk(wb_ref, taps - 1 - delay)
        acc = [acc[r] + wv * win[r] for r in range(CONV_RB)]
        if delay + 1 < taps:
            win = [blk(buf_ref, taps - 1 + i0 - delay - 1)] + win[:-1]
    for r in range(CONV_RB):
        out_ref[lane_group, SUBLANES * (i0 + r):SUBLANES * (i0 + r + 1), :] = acc[r]


def _conv_lane_group(wb_ref, buf_ref, out_ref, taps, lane_group):
    for rb in range(SEG // CONV_RB):
        _conv_block(wb_ref, buf_ref, out_ref, taps, lane_group, rb)


def _sublane_affine_scan(a, b):
    row = lax.broadcasted_iota(jnp.int32, a.shape, 0)
    for k in (1, 2, 4):
        keep = row >= k
        b = jnp.where(keep, a * pltpu.roll(b, k, 0) + b, b)
        a = jnp.where(keep, a * pltpu.roll(a, k, 0), a)
    return a, b


def _mixer_body(*refs, n_col):
    _mixer_tile(refs[:n_col], *refs[n_col:])


def _mixer_tile(x_refs, mixn_ref, w_lru_ref, w_cm_ref, w_qk_ref, w_v_ref, w_og_ref,
                w_gl_ref, w_mgab_ref, w_mgc_ref,
                lcw_ref, lcb_ref, wa_ref, ba_ref, wx_ref, bx_ref, lam_ref, lwo_ref,
                ccw_ref, ccb_ref, lng_ref, lnb_ref, cwo_ref,
                wg2_ref, bg_ref, gn_ref, gwo_ref, gb_ref, wout_ref,
                o_ref,
                xs_ref, hb_ref, hbi_ref, ubuf_ref, utail_ref, uwb_ref, uout_ref,
                cbuf_ref, ctail_ref, cwb_ref, cout_ref,
                a_ref, b_ref, hs_ref, mab_ref, qk_ref, v_ref, og_ref, gc_ref, gab_ref, ya_ref,
                hst_ref, s_ref):
    rows, d = o_ref.shape
    lru_taps = lcw_ref.shape[0]
    cm_taps = ccw_ref.shape[0]
    lane_groups = d // LANES
    dk = wg2_ref.shape[1] // GLA_HEADS
    dv = gwo_ref.shape[0] // GLA_HEADS

    @pl.when(pl.program_id(1) == 0)
    def _():
        utail_ref[...] = jnp.zeros(utail_ref.shape, F32)
        ctail_ref[...] = jnp.zeros(ctail_ref.shape, F32)
        hst_ref[...] = jnp.zeros(hst_ref.shape, F32)
        s_ref[...] = jnp.zeros(s_ref.shape, F32)
        _conv_weights_stage(lcw_ref, lcb_ref, uwb_ref)
        _conv_weights_stage(ccw_ref, ccb_ref, cwb_ref)

    x = jnp.concatenate([ref[...] for ref in x_refs], axis=1)
    hb_ref[...] = _rms_scale(x, mixn_ref[...]).astype(BF16)
    hbi_ref[...] = _rms_scale(_load_interleaved(x_refs, xs_ref), mixn_ref[...]).astype(BF16)

    _conv_stage(_dot(hbi_ref[...], w_lru_ref[...]), lru_taps, ubuf_ref, utail_ref)
    val = _dot(hbi_ref[...], w_cm_ref[:, 0:d])
    for c in range(lane_groups):
        _conv_lane_group(uwb_ref, ubuf_ref, uout_ref, lru_taps, c)
    gate = _dot(hbi_ref[...], w_cm_ref[:, d:2 * d])
    u = _lane_concat(uout_ref)
    ub = u.astype(BF16)
    lam = lam_ref[...]
    cvec = -LRU_C * (jnp.maximum(-lam, 0.0) + jnp.log1p(jnp.exp(-jnp.abs(lam))))
    blk = d // LRU_HEADS
    for h in range(LRU_HEADS):
        sl = slice(h * blk, (h + 1) * blk)
        r = _sigmoid(_dot(ub[:, sl], wa_ref[h]) + ba_ref[:, sl])
        i = _sigmoid(_dot(ub[:, sl], wx_ref[h]) + bx_ref[:, sl])
        a = jnp.exp(r * cvec[:, sl])
        a_ref[:, sl] = a
        b_ref[:, sl] = jnp.sqrt(1.0 - a * a) * (i * u[:, sl])
    _conv_stage(val * _sigmoid(gate), cm_taps, cbuf_ref, ctail_ref)
    prod = lin = None
    for i in range(SEG):
        rs = slice(SUBLANES * i, SUBLANES * (i + 1))
        a = a_ref[rs, :]
        b = b_ref[rs, :]
        if i == 0:
            prod, lin = a, b
        else:
            lin = a * lin + b
            prod = a * prod
        a_ref[rs, :] = prod
        b_ref[rs, :] = lin
    h_prev = hst_ref[...]
    run_a, run_b = _sublane_affine_scan(prod, lin)
    run_end = run_a * h_prev + run_b
    row8 = lax.broadcasted_iota(jnp.int32, run_end.shape, 0)
    run_in = jnp.where(row8 == 0, h_prev, pltpu.roll(run_end, 1, 0))
    hst_ref[...] = jnp.broadcast_to(run_end[SUBLANES - 1:SUBLANES, :], run_end.shape)
    hs_ref[...] = (b_ref[...] + a_ref[...] * jnp.tile(run_in, (SEG, 1))).astype(BF16)

    groups_per_step = lane_groups // PAIR_STEPS

    def paired_step(j, carry):
        for cc in range(groups_per_step):
            _conv_lane_group(cwb_ref, cbuf_ref, cout_ref, cm_taps, j * groups_per_step + cc)
        qk_ref[j] = _dot(hb_ref[...], w_qk_ref[j])
        v_ref[j] = _dot(hb_ref[...], w_v_ref[j]).astype(BF16)
        og_ref[j] = _dot(hb_ref[...], w_og_ref[j])
        gc_ref[j] = _dot(hb_ref[...], w_mgc_ref[j])
        gab_ref[2 * j] = _dot(hbi_ref[...], w_mgab_ref[2 * j])
        gab_ref[2 * j + 1] = _dot(hbi_ref[...], w_mgab_ref[2 * j + 1])
        ya_ref[j] = _dot(hs_ref[...], lwo_ref[j])
        return carry

    lax.fori_loop(0, PAIR_STEPS, paired_step, 0)

    acc = _lane_concat(cout_ref)
    mu = jnp.mean(acc, axis=-1, keepdims=True)
    xc = acc - mu
    var = jnp.mean(xc * xc, axis=-1, keepdims=True)
    ln = xc * lax.rsqrt(var + EPS) * lng_ref[...] + lnb_ref[...]
    glow = _dot(hb_ref[...], w_gl_ref[...]).astype(BF16)
    y_b = _dot((ln * _sigmoid(ln)).astype(BF16), cwo_ref[...])

    g_ab = _sigmoid(_lane_concat(gab_ref) + gb_ref[:, 0:2 * d])
    merged_i = g_ab[:, 0:d] * _lane_concat(ya_ref) + g_ab[:, d:2 * d] * y_b
    for c in range(lane_groups):
        mab_ref[c] = merged_i[:, c * LANES:(c + 1) * LANES]
    merged_ab = _load_natural(mab_ref)

    qk = _lane_concat(qk_ref)
    q = qk[:, 0:GLA_HEADS * dk]
    kk = qk[:, GLA_HEADS * dk:2 * GLA_HEADS * dk]
    z = _dot(glow, wg2_ref[...]) + bg_ref[...]
    la = (jnp.minimum(z, 0.0) - jnp.log1p(jnp.exp(-jnp.abs(z)))) * (1.0 / GLA_TAU)
    ri = lax.broadcasted_iota(jnp.int32, (rows, rows), 0)
    ci = lax.broadcasted_iota(jnp.int32, (rows, rows), 1)
    causal = (ri >= ci) & ((ri // GLA_CHUNK) == (ci // GLA_CHUNK))
    tri = jnp.where(causal, 1.0, 0.0).astype(BF16)
    la_hi = la.astype(BF16)
    la_lo = (la - la_hi.astype(F32)).astype(BF16)
    bcum = _dot(tri, la_hi) + _dot(tri, la_lo)
    n_chunks = rows // GLA_CHUNK
    blast = [bcum[(c + 1) * GLA_CHUNK - 1:(c + 1) * GLA_CHUNK, :] for c in range(n_chunks)]
    blast_full = jnp.concatenate(
        [jnp.broadcast_to(bl, (GLA_CHUNK, bl.shape[1])) for bl in blast], axis=0)
    q_dec = (q * (dk ** -0.5) * jnp.exp(bcum)).astype(BF16)
    k_intra = (kk * jnp.exp(-bcum)).astype(BF16)
    k_state = (kk * jnp.exp(blast_full - bcum)).astype(BF16)

    heads = []
    for h in range(GLA_HEADS):
        ks = slice(h * dk, (h + 1) * dk)
        v_h = v_ref[h]
        sc = lax.dot_general(q_dec[:, ks], k_intra[:, ks], (((1,), (1,)), ((), ())),
                             preferred_element_type=F32)
        o_h = _dot(jnp.where(causal, sc, 0.0).astype(BF16), v_h)
        st = s_ref[h]
        inter = []
        for c in range(n_chunks):
            rs = slice(c * GLA_CHUNK, (c + 1) * GLA_CHUNK)
            inter.append(lax.dot_general(q_dec[rs, ks], st.astype(BF16),
                                         (((1,), (1,)), ((), ())), preferred_element_type=F32))
            st = st * jnp.exp(blast[c][:, ks]) + lax.dot_general(
                v_h[rs, :], k_state[rs, ks], (((0,), (0,)), ((), ())),
                preferred_element_type=F32)
        s_ref[h] = st
        o_h = o_h + jnp.concatenate(inter, axis=0)
        o_h = _rms_scale(o_h, gn_ref[...])
        og_h = og_ref[h]
        heads.append((o_h * (og_h * _sigmoid(og_h))).astype(BF16))
    y_c = _dot(jnp.concatenate(heads, axis=1), gwo_ref[...])

    g_c = _sigmoid(_lane_concat(gc_ref) + gb_ref[:, 2 * d:3 * d])
    merged = merged_ab + g_c * y_c
    o_ref[...] = x + _dot(merged.astype(BF16), wout_ref[...])


_MIXER_PARAMS = [
    "mixn", "w_lru", "w_cm", "w_qk", "w_v", "w_og", "w_gl", "w_mgab", "w_mgc",
    "lcw", "lcb", "wa", "ba", "wx", "bx", "lam", "lwo",
    "ccw", "ccb", "lng", "lnb", "cwo",
    "wg2", "bg", "gn", "gwo", "gb", "wout",
]


def _col_chunks(w):
    k, n = w.shape
    return w.reshape(k, n // MXU_COLS, MXU_COLS).transpose(1, 0, 2)


def _mixer(x2, seq_len, weights):
    n, d = x2.shape
    assert seq_len % MIX_ROWS == 0 and MIX_ROWS % GLA_CHUNK == 0 and SEG % SUBLANES == 0
    tiles = seq_len // MIX_ROWS
    batch = n // seq_len
    full = lambda a: pl.BlockSpec(a.shape, lambda b, j: (0,) * a.ndim)
    row_spec = pl.BlockSpec((MIX_ROWS, d), lambda b, j: (b * tiles + j, 0))
    n_col = d // LANES
    col_specs = [pl.BlockSpec((MIX_ROWS, LANES), lambda b, j, c=c: (b * tiles + j, c))
                 for c in range(n_col)]
    assert list(weights) == _MIXER_PARAMS
    lru_taps = weights["lcw"].shape[0]
    cm_taps = weights["ccw"].shape[0]
    assert max(lru_taps, cm_taps) <= SEG and n_col % PAIR_STEPS == 0
    dk = weights["wg2"].shape[1] // GLA_HEADS
    dv = weights["gwo"].shape[0] // GLA_HEADS
    chunks = d // MXU_COLS
    assert chunks == PAIR_STEPS and dv == MXU_COLS and 2 * GLA_HEADS * dk == chunks * MXU_COLS
    conv_scratch = lambda taps: [
        pltpu.VMEM((n_col, MIX_ROWS + SUBLANES * (taps - 1), LANES), F32),
        pltpu.VMEM((n_col, SUBLANES * (taps - 1), LANES), F32),
        pltpu.VMEM((n_col, SUBLANES * (taps + 1), LANES), F32),
        pltpu.VMEM((n_col, MIX_ROWS, LANES), F32),
    ]
    return pl.pallas_call(
        functools.partial(_mixer_body, n_col=n_col),
        grid=(batch, tiles),
        in_specs=col_specs + [full(w) for w in weights.values()],
        out_specs=row_spec,
        out_shape=jax.ShapeDtypeStruct((n, d), F32),
        scratch_shapes=[
            pltpu.VMEM((n_col, SUBLANES * SEG_PITCH, LANES), F32),
            pltpu.VMEM((MIX_ROWS, d), BF16),
            pltpu.VMEM((MIX_ROWS, d), BF16),
            *conv_scratch(lru_taps),
            *conv_scratch(cm_taps),
            pltpu.VMEM((MIX_ROWS, d), F32),
            pltpu.VMEM((MIX_ROWS, d), F32),
            pltpu.VMEM((MIX_ROWS, d), BF16),
            pltpu.VMEM((n_col, MIX_ROWS, LANES), F32),
            pltpu.VMEM((chunks, MIX_ROWS, MXU_COLS), F32),
            pltpu.VMEM((chunks, MIX_ROWS, MXU_COLS), BF16),
            pltpu.VMEM((chunks, MIX_ROWS, MXU_COLS), F32),
            pltpu.VMEM((chunks, MIX_ROWS, MXU_COLS), F32),
            pltpu.VMEM((2 * chunks, MIX_ROWS, MXU_COLS), F32),
            pltpu.VMEM((chunks, MIX_ROWS, MXU_COLS), F32),
            pltpu.VMEM((SUBLANES, d), F32),
            pltpu.VMEM((GLA_HEADS, dv, dk), F32),
        ],
        compiler_params=pltpu.CompilerParams(
            dimension_semantics=("arbitrary", "arbitrary"), vmem_limit_bytes=VMEM_LIMIT_BYTES),
        name="mixer",
    )(*([x2] * n_col), *weights.values())


def _row(v):
    return v.reshape(1, -1).astype(F32)


def kernel(x, ffn1_norm, ffn1_w_gate, ffn1_w_up, ffn1_w_down, mix_norm, w_in, lru_conv_w, lru_conv_b, lru_w_a, lru_b_a, lru_w_x, lru_b_x, lru_lambda, lru_w_o, cm_conv_w, cm_conv_b, cm_ln_g, cm_ln_b, cm_w_o, gla_w_g2, gla_b_g, gla_norm, gla_w_o, gate_b, w_out, ffn2_norm, ffn2_w_gate, ffn2_w_up, ffn2_w_down, final_norm):
    bsz, t, d = x.shape
    depth = w_in.shape[0]
    lru_w = lru_w_o.shape[1]
    cm_w = cm_w_o.shape[1]
    qk_w = gla_w_g2.shape[2]
    v_w = gla_w_o.shape[1]
    rank = gla_w_g2.shape[1]
    widths = (lru_w, 2 * cm_w, qk_w, qk_w, v_w, v_w, rank, gate_b.shape[1])
    offs = [0]
    for w in widths:
        offs.append(offs[-1] + w)
    assert offs[-1] == w_in.shape[2]
    rank_pad = -(-rank // LANES) * LANES

    x2 = x.reshape(bsz * t, d)
    fin = _row(final_norm)
    for i in range(depth):
        x2 = _ffn(x2, _row(ffn1_norm[i]), ffn1_w_gate[i].astype(BF16), ffn1_w_up[i].astype(BF16),
                  ffn1_w_down[i].astype(BF16), fin, final=False)
        wi = w_in[i].astype(BF16)
        seg = [wi[:, offs[s]:offs[s + 1]] for s in range(len(widths))]
        w_gl = jnp.pad(seg[6], ((0, 0), (0, rank_pad - rank)))
        w_g2 = jnp.pad(gla_w_g2[i].astype(BF16), ((0, rank_pad - rank), (0, 0)))
        weights = dict(
            mixn=_row(mix_norm[i]), w_lru=seg[0], w_cm=seg[1],
            w_qk=_col_chunks(wi[:, offs[2]:offs[4]]), w_v=_col_chunks(seg[4]),
            w_og=_col_chunks(seg[5]), w_gl=w_gl,
            w_mgab=_col_chunks(seg[7][:, 0:2 * d]), w_mgc=_col_chunks(seg[7][:, 2 * d:3 * d]),
            lcw=lru_conv_w[i], lcb=_row(lru_conv_b[i]), wa=lru_w_a[i].astype(BF16),
            ba=_row(lru_b_a[i]), wx=lru_w_x[i].astype(BF16), bx=_row(lru_b_x[i]),
            lam=_row(lru_lambda[i]), lwo=_col_chunks(lru_w_o[i].astype(BF16)),
            ccw=cm_conv_w[i], ccb=_row(cm_conv_b[i]), lng=_row(cm_ln_g[i]), lnb=_row(cm_ln_b[i]),
            cwo=cm_w_o[i].astype(BF16),
            wg2=w_g2, bg=_row(gla_b_g[i]), gn=_row(gla_norm[i]), gwo=gla_w_o[i].astype(BF16),
            gb=_row(gate_b[i]), wout=w_out[i].astype(BF16),
        )
        x2 = _mixer(x2, t, weights)
        x2 = _ffn(x2, _row(ffn2_norm[i]), ffn2_w_gate[i].astype(BF16), ffn2_w_up[i].astype(BF16),
                  ffn2_w_down[i].astype(BF16), fin, final=(i == depth - 1))
    return x2.reshape(bsz, t, d)
```

```python
import functools

import jax
import jax.numpy as jnp
from jax import lax
from jax.experimental import pallas as pl
from jax.experimental.pallas import tpu as pltpu

F32 = jnp.float32
BF16 = jnp.bfloat16

EPS = 1e-6
LRU_HEADS = 4
LRU_C = 8.0
GLA_HEADS = 4
GLA_TAU = 16.0
GLA_CHUNK = 64

LANES = 128
SUBLANES = 8
MXU_COLS = 256
VMEM_LIMIT_BYTES = 56 * 1024 * 1024

FFN_ROWS = 1024
MIX_ROWS = 256
SEG = MIX_ROWS // SUBLANES
SEG_PITCH = SEG + SUBLANES
CONV_RB = 8


def _dot(a, b):
    return jnp.dot(a, b, preferred_element_type=F32)


def _sigmoid(x):
    return jax.nn.sigmoid(x)


def _rms_scale(x, g):
    ms = jnp.mean(x * x, axis=-1, keepdims=True)
    return x * lax.rsqrt(ms + EPS) * g


def _ffn_body(x_ref, g_ref, wg_ref, wu_ref, wd_ref, fin_ref, o_ref, act_ref, *, final):
    x = x_ref[...]
    h = _rms_scale(x, g_ref[...]).astype(BF16)
    d_ff = wg_ref.shape[1]
    for c in range(d_ff // MXU_COLS):
        sl = slice(c * MXU_COLS, (c + 1) * MXU_COLS)
        gate = _dot(h, wg_ref[:, sl])
        up = _dot(h, wu_ref[:, sl])
        act_ref[:, sl] = (gate * _sigmoid(gate) * up).astype(BF16)
    y = x + 0.5 * _dot(act_ref[...], wd_ref[...])
    if final:
        y = _rms_scale(y, fin_ref[...])
    o_ref[...] = y


def _ffn(x2, g, wg, wu, wd, fin, *, final):
    n, d = x2.shape
    d_ff = wg.shape[1]
    assert n % FFN_ROWS == 0 and d_ff % MXU_COLS == 0
    full = lambda a: pl.BlockSpec(a.shape, lambda i: (0,) * a.ndim)
    return pl.pallas_call(
        functools.partial(_ffn_body, final=final),
        grid=(n // FFN_ROWS,),
        in_specs=[pl.BlockSpec((FFN_ROWS, d), lambda i: (i, 0)),
                  full(g), full(wg), full(wu), full(wd), full(fin)],
        out_specs=pl.BlockSpec((FFN_ROWS, d), lambda i: (i, 0)),
        out_shape=jax.ShapeDtypeStruct((n, d), F32),
        scratch_shapes=[pltpu.VMEM((FFN_ROWS, d_ff), BF16)],
        compiler_params=pltpu.CompilerParams(
            dimension_semantics=("arbitrary",), vmem_limit_bytes=VMEM_LIMIT_BYTES),
        name="ffn_final" if final else "ffn",
    )(x2, g, wg, wu, wd, fin)


def _load_interleaved(col_refs, xs_ref):
    for c, ref in enumerate(col_refs):
        for j in range(SUBLANES):
            xs_ref[c, SEG_PITCH * j:SEG_PITCH * j + SEG, :] = ref[SEG * j:SEG * (j + 1), :]
    return jnp.concatenate(
        [jnp.concatenate([xs_ref[c, pl.ds(i, SUBLANES, stride=SEG_PITCH), :]
                          for c in range(len(col_refs))], axis=1)
         for i in range(SEG)], axis=0)


def _load_natural(ref):
    per = SEG // SUBLANES
    return jnp.concatenate(
        [jnp.concatenate(
            [ref[c, pl.ds(SUBLANES * SUBLANES * (r % per) + r // per, SUBLANES, stride=SUBLANES), :]
             for c in range(ref.shape[0])], axis=1)
         for r in range(SEG)], axis=0)


def _lane_concat(ref):
    return jnp.concatenate([ref[c] for c in range(ref.shape[0])], axis=1)


def _conv_stage(cur, taps, buf_ref, tail_ref):
    rows = cur.shape[0]
    hist = SUBLANES * (taps - 1)
    sub = lax.broadcasted_iota(jnp.int32, (taps - 1, SUBLANES, LANES), 1)
    for c in range(buf_ref.shape[0]):
        buf_ref[c, hist:hist + rows, :] = cur[:, LANES * c:LANES * (c + 1)]
        new_tail = buf_ref[c, rows:rows + hist, :]
        from_prev = pltpu.roll(tail_ref[c].reshape(taps - 1, SUBLANES, LANES), 1, 1)
        from_cur = pltpu.roll(new_tail.reshape(taps - 1, SUBLANES, LANES), 1, 1)
        buf_ref[c, 0:hist, :] = jnp.where(sub == 0, from_prev, from_cur).reshape(hist, LANES)
        tail_ref[c] = new_tail


def _conv_weights_stage(w_ref, bias_ref, wb_ref):
    taps = w_ref.shape[0]
    for c in range(wb_ref.shape[0]):
        cs = slice(LANES * c, LANES * (c + 1))
        for k in range(taps):
            wb_ref[c, SUBLANES * k:SUBLANES * (k + 1), :] = jnp.broadcast_to(
                w_ref[k:k + 1, cs], (SUBLANES, LANES))
        wb_ref[c, SUBLANES * taps:SUBLANES * (taps + 1), :] = jnp.broadcast_to(
            bias_ref[:, cs], (SUBLANES, LANES))


def _conv_block(wb_ref, buf_ref, out_ref, taps, lane_group, row_block):
    i0 = row_block * CONV_RB
    blk = lambda ref, i: ref[lane_group, SUBLANES * i:SUBLANES * (i + 1), :]
    win = [blk(buf_ref, taps - 1 + i0 + r) for r in range(CONV_RB)]
    bias = blk(wb_ref, taps)
    acc = [bias] * CONV_RB
    for delay in range(taps):
        wv = blk(wb_ref, taps - 1 - delay)
        acc = [acc[r] + wv * win[r] for r in range(CONV_RB)]
        if delay + 1 < taps:
            win = [blk(buf_ref, taps - 1 + i0 - delay - 1)] + win[:-1]
    for r in range(CONV_RB):
        out_ref[lane_group, SUBLANES * (i0 + r):SUBLANES * (i0 + r + 1), :] = acc[r]


def _conv_lane_group(wb_ref, buf_ref, out_ref, taps, lane_group):
    for rb in range(SEG // CONV_RB):
        _conv_block(wb_ref, buf_ref, out_ref, taps, lane_group, rb)


def _sublane_affine_scan(a, b):
    row = lax.broadcasted_iota(jnp.int32, a.shape, 0)
    for k in (1, 2, 4):
        keep = row >= k
        b = jnp.where(keep, a * pltpu.roll(b, k, 0) + b, b)
        a = jnp.where(keep, a * pltpu.roll(a, k, 0), a)
    return a, b


def _mixer_body(*refs, n_col):
    _mixer_tile(refs[:n_col], *refs[n_col:])


def _mixer_tile(x_refs, mixn_ref, w_lru_ref, w_cm_ref, w_qk_ref, w_v_ref, w_og_ref,
                w_gl_ref, w_mgab_ref, w_mgc_ref,
                lcw_ref, lcb_ref, wa_ref, ba_ref, wx_ref, bx_ref, lam_ref, lwo_ref,
                ccw_ref, ccb_ref, lng_ref, lnb_ref, cwo_ref,
                wg2_ref, bg_ref, gn_ref, gwo_ref, gb_ref, wout_ref,
                o_ref,
                xs_ref, hb_ref, hbi_ref, ubuf_ref, utail_ref, uwb_ref, uout_ref,
                cbuf_ref, ctail_ref, cwb_ref, cout_ref,
                a_ref, b_ref, hs_ref, mab_ref, qk_ref, v_ref, og_ref, gc_ref, gab_ref, ya_ref,
                hst_ref, s_ref):
    rows, d = o_ref.shape
    lru_taps = lcw_ref.shape[0]
    cm_taps = ccw_ref.shape[0]
    lane_groups = d // LANES
    dk = wg2_ref.shape[1] // GLA_HEADS
    dv = gwo_ref.shape[0] // GLA_HEADS

    @pl.when(pl.program_id(1) == 0)
    def _():
        utail_ref[...] = jnp.zeros(utail_ref.shape, F32)
        ctail_ref[...] = jnp.zeros(ctail_ref.shape, F32)
        hst_ref[...] = jnp.zeros(hst_ref.shape, F32)
        s_ref[...] = jnp.zeros(s_ref.shape, F32)
        _conv_weights_stage(lcw_ref, lcb_ref, uwb_ref)
        _conv_weights_stage(ccw_ref, ccb_ref, cwb_ref)

    x = jnp.concatenate([ref[...] for ref in x_refs], axis=1)
    hb_ref[...] = _rms_scale(x, mixn_ref[...]).astype(BF16)
    hbi_ref[...] = _rms_scale(_load_interleaved(x_refs, xs_ref), mixn_ref[...]).astype(BF16)

    _conv_stage(_dot(hbi_ref[...], w_lru_ref[...]), lru_taps, ubuf_ref, utail_ref)
    val = _dot(hbi_ref[...], w_cm_ref[:, 0:d])
    for c in range(lane_groups):
        _conv_lane_group(uwb_ref, ubuf_ref, uout_ref, lru_taps, c)
    gate = _dot(hbi_ref[...], w_cm_ref[:, d:2 * d])
    u = _lane_concat(uout_ref)
    ub = u.astype(BF16)
    lam = lam_ref[...]
    cvec = -LRU_C * (jnp.maximum(-lam, 0.0) + jnp.log1p(jnp.exp(-jnp.abs(lam))))
    blk = d // LRU_HEADS
    for h in range(LRU_HEADS):
        sl = slice(h * blk, (h + 1) * blk)
        r = _sigmoid(_dot(ub[:, sl], wa_ref[h]) + ba_ref[:, sl])
        i = _sigmoid(_dot(ub[:, sl], wx_ref[h]) + bx_ref[:, sl])
        a = jnp.exp(r * cvec[:, sl])
        a_ref[:, sl] = a
        b_ref[:, sl] = jnp.sqrt(1.0 - a * a) * (i * u[:, sl])
    _conv_stage(val * _sigmoid(gate), cm_taps, cbuf_ref, ctail_ref)
    prod = lin = None
    for i in range(SEG):
        rs = slice(SUBLANES * i, SUBLANES * (i + 1))
        a = a_ref[rs, :]
        b = b_ref[rs, :]
        if i == 0:
            prod, lin = a, b
        else:
            lin = a * lin + b
            prod = a * prod
        a_ref[rs, :] = prod
        b_ref[rs, :] = lin
    h_prev = hst_ref[...]
    run_a, run_b = _sublane_affine_scan(prod, lin)
    run_end = run_a * h_prev + run_b
    row8 = lax.broadcasted_iota(jnp.int32, run_end.shape, 0)
    run_in = jnp.where(row8 == 0, h_prev, pltpu.roll(run_end, 1, 0))
    hst_ref[...] = jnp.broadcast_to(run_end[SUBLANES - 1:SUBLANES, :], run_end.shape)
    hs_ref[...] = (b_ref[...] + a_ref[...] * jnp.tile(run_in, (SEG, 1))).astype(BF16)

    def conv_step(c, carry):
        _conv_lane_group(cwb_ref, cbuf_ref, cout_ref, cm_taps, c)
        return carry

    lax.fori_loop(0, lane_groups, conv_step, 0)
    for j in range(qk_ref.shape[0]):
        qk_ref[j] = _dot(hb_ref[...], w_qk_ref[j])
        v_ref[j] = _dot(hb_ref[...], w_v_ref[j]).astype(BF16)
        og_ref[j] = _dot(hb_ref[...], w_og_ref[j])
        gc_ref[j] = _dot(hb_ref[...], w_mgc_ref[j])
        gab_ref[2 * j] = _dot(hbi_ref[...], w_mgab_ref[2 * j])
        gab_ref[2 * j + 1] = _dot(hbi_ref[...], w_mgab_ref[2 * j + 1])
        ya_ref[j] = _dot(hs_ref[...], lwo_ref[j])

    acc = _lane_concat(cout_ref)
    mu = jnp.mean(acc, axis=-1, keepdims=True)
    xc = acc - mu
    var = jnp.mean(xc * xc, axis=-1, keepdims=True)
    ln = xc * lax.rsqrt(var + EPS) * lng_ref[...] + lnb_ref[...]
    glow = _dot(hb_ref[...], w_gl_ref[...]).astype(BF16)
    y_b = _dot((ln * _sigmoid(ln)).astype(BF16), cwo_ref[...])

    g_ab = _sigmoid(_lane_concat(gab_ref) + gb_ref[:, 0:2 * d])
    merged_i = g_ab[:, 0:d] * _lane_concat(ya_ref) + g_ab[:, d:2 * d] * y_b
    for c in range(lane_groups):
        mab_ref[c] = merged_i[:, c * LANES:(c + 1) * LANES]
    merged_ab = _load_natural(mab_ref)

    qk = _lane_concat(qk_ref)
    q = qk[:, 0:GLA_HEADS * dk]
    kk = qk[:, GLA_HEADS * dk:2 * GLA_HEADS * dk]
    z = _dot(glow, wg2_ref[...]) + bg_ref[...]
    la = (jnp.minimum(z, 0.0) - jnp.log(1.0 + jnp.exp(-jnp.abs(z)))) * (1.0 / GLA_TAU)
    ri = lax.broadcasted_iota(jnp.int32, (rows, rows), 0)
    ci = lax.broadcasted_iota(jnp.int32, (rows, rows), 1)
    causal = (ri >= ci) & ((ri // GLA_CHUNK) == (ci // GLA_CHUNK))
    tri = jnp.where(causal, 1.0, 0.0).astype(BF16)
    la_hi = la.astype(BF16)
    la_lo = (la - la_hi.astype(F32)).astype(BF16)
    bcum = _dot(tri, la_hi) + _dot(tri, la_lo)
    n_chunks = rows // GLA_CHUNK
    blast = [bcum[(c + 1) * GLA_CHUNK - 1:(c + 1) * GLA_CHUNK, :] for c in range(n_chunks)]
    blast_full = jnp.concatenate(
        [jnp.broadcast_to(bl, (GLA_CHUNK, bl.shape[1])) for bl in blast], axis=0)
    q_dec = (q * (dk ** -0.5) * jnp.exp(bcum)).astype(BF16)
    k_intra = (kk * jnp.exp(-bcum)).astype(BF16)
    k_state = (kk * jnp.exp(blast_full - bcum)).astype(BF16)

    heads = []
    for h in range(GLA_HEADS):
        ks = slice(h * dk, (h + 1) * dk)
        v_h = v_ref[h]
        sc = lax.dot_general(q_dec[:, ks], k_intra[:, ks], (((1,), (1,)), ((), ())),
                             preferred_element_type=F32)
        o_h = _dot(jnp.where(causal, sc, 0.0).astype(BF16), v_h)
        st = s_ref[h]
        inter = []
        for c in range(n_chunks):
            rs = slice(c * GLA_CHUNK, (c + 1) * GLA_CHUNK)
            inter.append(lax.dot_general(q_dec[rs, ks], st.astype(BF16),
                                         (((1,), (1,)), ((), ())), preferred_element_type=F32))
            st = st * jnp.exp(blast[c][:, ks]) + lax.dot_general(
                v_h[rs, :], k_state[rs, ks], (((0,), (0,)), ((), ())),
                preferred_element_type=F32)
        s_ref[h] = st
        o_h = o_h + jnp.concatenate(inter, axis=0)
        o_h = _rms_scale(o_h, gn_ref[...])
        og_h = og_ref[h]
        heads.append((o_h * (og_h * _sigmoid(og_h))).astype(BF16))
    y_c = _dot(jnp.concatenate(heads, axis=1), gwo_ref[...])

    g_c = _sigmoid(_lane_concat(gc_ref) + gb_ref[:, 2 * d:3 * d])
    merged = merged_ab + g_c * y_c
    o_ref[...] = x + _dot(merged.astype(BF16), wout_ref[...])


_MIXER_PARAMS = [
    "mixn", "w_lru", "w_cm", "w_qk", "w_v", "w_og", "w_gl", "w_mgab", "w_mgc",
    "lcw", "lcb", "wa", "ba", "wx", "bx", "lam", "lwo",
    "ccw", "ccb", "lng", "lnb", "cwo",
    "wg2", "bg", "gn", "gwo", "gb", "wout",
]


def _col_chunks(w):
    k, n = w.shape
    return w.reshape(k, n // MXU_COLS, MXU_COLS).transpose(1, 0, 2)


def _mixer(x2, seq_len, weights):
    n, d = x2.shape
    assert seq_len % MIX_ROWS == 0 and MIX_ROWS % GLA_CHUNK == 0 and SEG % SUBLANES == 0
    tiles = seq_len // MIX_ROWS
    batch = n // seq_len
    full = lambda a: pl.BlockSpec(a.shape, lambda b, j: (0,) * a.ndim)
    row_spec = pl.BlockSpec((MIX_ROWS, d), lambda b, j: (b * tiles + j, 0))
    n_col = d // LANES
    col_specs = [pl.BlockSpec((MIX_ROWS, LANES), lambda b, j, c=c: (b * tiles + j, c))
                 for c in range(n_col)]
    assert list(weights) == _MIXER_PARAMS
    lru_taps = weights["lcw"].shape[0]
    cm_taps = weights["ccw"].shape[0]
    assert max(lru_taps, cm_taps) <= SEG
    dk = weights["wg2"].shape[1] // GLA_HEADS
    dv = weights["gwo"].shape[0] // GLA_HEADS
    chunks = d // MXU_COLS
    assert dv == MXU_COLS and 2 * GLA_HEADS * dk == chunks * MXU_COLS
    conv_scratch = lambda taps: [
        pltpu.VMEM((n_col, MIX_ROWS + SUBLANES * (taps - 1), LANES), F32),
        pltpu.VMEM((n_col, SUBLANES * (taps - 1), LANES), F32),
        pltpu.VMEM((n_col, SUBLANES * (taps + 1), LANES), F32),
        pltpu.VMEM((n_col, MIX_ROWS, LANES), F32),
    ]
    return pl.pallas_call(
        functools.partial(_mixer_body, n_col=n_col),
        grid=(batch, tiles),
        in_specs=col_specs + [full(w) for w in weights.values()],
        out_specs=row_spec,
        out_shape=jax.ShapeDtypeStruct((n, d), F32),
        scratch_shapes=[
            pltpu.VMEM((n_col, SUBLANES * SEG_PITCH, LANES), F32),
            pltpu.VMEM((MIX_ROWS, d), BF16),
            pltpu.VMEM((MIX_ROWS, d), BF16),
            *conv_scratch(lru_taps),
            *conv_scratch(cm_taps),
            pltpu.VMEM((MIX_ROWS, d), F32),
            pltpu.VMEM((MIX_ROWS, d), F32),
            pltpu.VMEM((MIX_ROWS, d), BF16),
            pltpu.VMEM((n_col, MIX_ROWS, LANES), F32),
            pltpu.VMEM((chunks, MIX_ROWS, MXU_COLS), F32),
            pltpu.VMEM((chunks, MIX_ROWS, MXU_COLS), BF16),
            pltpu.VMEM((chunks, MIX_ROWS, MXU_COLS), F32),
            pltpu.VMEM((chunks, MIX_ROWS, MXU_COLS), F32),
            pltpu.VMEM((2 * chunks, MIX_ROWS, MXU_COLS), F32),
            pltpu.VMEM((chunks, MIX_ROWS, MXU_COLS), F32),
            pltpu.VMEM((SUBLANES, d), F32),
            pltpu.VMEM((GLA_HEADS, dv, dk), F32),
        ],
        compiler_params=pltpu.CompilerParams(
            dimension_semantics=("arbitrary", "arbitrary"), vmem_limit_bytes=VMEM_LIMIT_BYTES),
        name="mixer",
    )(*([x2] * n_col), *weights.values())


def _row(v):
    return v.reshape(1, -1).astype(F32)


def kernel(x, ffn1_norm, ffn1_w_gate, ffn1_w_up, ffn1_w_down, mix_norm, w_in, lru_conv_w, lru_conv_b, lru_w_a, lru_b_a, lru_w_x, lru_b_x, lru_lambda, lru_w_o, cm_conv_w, cm_conv_b, cm_ln_g, cm_ln_b, cm_w_o, gla_w_g2, gla_b_g, gla_norm, gla_w_o, gate_b, w_out, ffn2_norm, ffn2_w_gate, ffn2_w_up, ffn2_w_down, final_norm):
    bsz, t, d = x.shape
    depth = w_in.shape[0]
    lru_w = lru_w_o.shape[1]
    cm_w = cm_w_o.shape[1]
    qk_w = gla_w_g2.shape[2]
    v_w = gla_w_o.shape[1]
    rank = gla_w_g2.shape[1]
    widths = (lru_w, 2 * cm_w, qk_w, qk_w, v_w, v_w, rank, gate_b.shape[1])
    offs = [0]
    for w in widths:
        offs.append(offs[-1] + w)
    assert offs[-1] == w_in.shape[2]
    rank_pad = -(-rank // LANES) * LANES

    x2 = x.reshape(bsz * t, d)
    fin = _row(final_norm)
    for i in range(depth):
        x2 = _ffn(x2, _row(ffn1_norm[i]), ffn1_w_gate[i].astype(BF16), ffn1_w_up[i].astype(BF16),
                  ffn1_w_down[i].astype(BF16), fin, final=False)
        wi = w_in[i].astype(BF16)
        seg = [wi[:, offs[s]:offs[s + 1]] for s in range(len(widths))]
        w_gl = jnp.pad(seg[6], ((0, 0), (0, rank_pad - rank)))
        w_g2 = jnp.pad(gla_w_g2[i].astype(BF16), ((0, rank_pad - rank), (0, 0)))
        weights = dict(
            mixn=_row(mix_norm[i]), w_lru=seg[0], w_cm=seg[1],
            w_qk=_col_chunks(wi[:, offs[2]:offs[4]]), w_v=_col_chunks(seg[4]),
            w_og=_col_chunks(seg[5]), w_gl=w_gl,
            w_mgab=_col_chunks(seg[7][:, 0:2 * d]), w_mgc=_col_chunks(seg[7][:, 2 * d:3 * d]),
            lcw=lru_conv_w[i], lcb=_row(lru_conv_b[i]), wa=lru_w_a[i].astype(BF16),
            ba=_row(lru_b_a[i]), wx=lru_w_x[i].astype(BF16), bx=_row(lru_b_x[i]),
            lam=_row(lru_lambda[i]), lwo=_col_chunks(lru_w_o[i].astype(BF16)),
            ccw=cm_conv_w[i], ccb=_row(cm_conv_b[i]), lng=_row(cm_ln_g[i]), lnb=_row(cm_ln_b[i]),
            cwo=cm_w_o[i].astype(BF16),
            wg2=w_g2, bg=_row(gla_b_g[i]), gn=_row(gla_norm[i]), gwo=gla_w_o[i].astype(BF16),
            gb=_row(gate_b[i]), wout=w_out[i].astype(BF16),
        )
        x2 = _mixer(x2, t, weights)
        x2 = _ffn(x2, _row(ffn2_norm[i]), ffn2_w_gate[i].astype(BF16), ffn2_w_up[i].astype(BF16),
                  ffn2_w_down[i].astype(BF16), fin, final=(i == depth - 1))
    return x2.reshape(bsz, t, d)
```

```python
import functools

import jax
import jax.numpy as jnp
from jax import lax
from jax.experimental import pallas as pl
from jax.experimental.pallas import tpu as pltpu

F32 = jnp.float32
BF16 = jnp.bfloat16

EPS = 1e-6
LRU_HEADS = 4
LRU_C = 8.0
GLA_HEADS = 4
GLA_TAU = 16.0
GLA_CHUNK = 64

LANES = 128
SUBLANES = 8
MXU_COLS = 256
VMEM_LIMIT_BYTES = 56 * 1024 * 1024

FFN_ROWS = 1024
MIX_ROWS = 256
SEG = MIX_ROWS // SUBLANES
SEG_PITCH = SEG + SUBLANES
CONV_RB = 8


def _dot(a, b):
    return jnp.dot(a, b, preferred_element_type=F32)


def _sigmoid(x):
    return jax.nn.sigmoid(x)


def _rms_scale(x, g):
    ms = jnp.mean(x * x, axis=-1, keepdims=True)
    return x * lax.rsqrt(ms + EPS) * g


def _ffn_body(x_ref, g_ref, wg_ref, wu_ref, wd_ref, fin_ref, o_ref, act_ref, *, final):
    x = x_ref[...]
    h = _rms_scale(x, g_ref[...]).astype(BF16)
    d_ff = wg_ref.shape[1]
    for c in range(d_ff // MXU_COLS):
        sl = slice(c * MXU_COLS, (c + 1) * MXU_COLS)
        gate = _dot(h, wg_ref[:, sl])
        up = _dot(h, wu_ref[:, sl])
        act_ref[:, sl] = (gate * _sigmoid(gate) * up).astype(BF16)
    y = x + 0.5 * _dot(act_ref[...], wd_ref[...])
    if final:
        y = _rms_scale(y, fin_ref[...])
    o_ref[...] = y


def _ffn(x2, g, wg, wu, wd, fin, *, final):
    n, d = x2.shape
    d_ff = wg.shape[1]
    assert n % FFN_ROWS == 0 and d_ff % MXU_COLS == 0
    full = lambda a: pl.BlockSpec(a.shape, lambda i: (0,) * a.ndim)
    return pl.pallas_call(
        functools.partial(_ffn_body, final=final),
        grid=(n // FFN_ROWS,),
        in_specs=[pl.BlockSpec((FFN_ROWS, d), lambda i: (i, 0)),
                  full(g), full(wg), full(wu), full(wd), full(fin)],
        out_specs=pl.BlockSpec((FFN_ROWS, d), lambda i: (i, 0)),
        out_shape=jax.ShapeDtypeStruct((n, d), F32),
        scratch_shapes=[pltpu.VMEM((FFN_ROWS, d_ff), BF16)],
        compiler_params=pltpu.CompilerParams(
            dimension_semantics=("arbitrary",), vmem_limit_bytes=VMEM_LIMIT_BYTES),
        name="ffn_final" if final else "ffn",
    )(x2, g, wg, wu, wd, fin)


def _load_interleaved(col_refs, xs_ref):
    for c, ref in enumerate(col_refs):
        for j in range(SUBLANES):
            xs_ref[c, SEG_PITCH * j:SEG_PITCH * j + SEG, :] = ref[SEG * j:SEG * (j + 1), :]
    return jnp.concatenate(
        [jnp.concatenate([xs_ref[c, pl.ds(i, SUBLANES, stride=SEG_PITCH), :]
                          for c in range(len(col_refs))], axis=1)
         for i in range(SEG)], axis=0)


def _load_natural(ref):
    per = SEG // SUBLANES
    return jnp.concatenate(
        [jnp.concatenate(
            [ref[c, pl.ds(SUBLANES * SUBLANES * (r % per) + r // per, SUBLANES, stride=SUBLANES), :]
             for c in range(ref.shape[0])], axis=1)
         for r in range(SEG)], axis=0)


def _lane_concat(ref):
    return jnp.concatenate([ref[c] for c in range(ref.shape[0])], axis=1)


def _conv_stage(cur, taps, buf_ref, tail_ref):
    rows = cur.shape[0]
    hist = SUBLANES * (taps - 1)
    sub = lax.broadcasted_iota(jnp.int32, (taps - 1, SUBLANES, LANES), 1)
    for c in range(buf_ref.shape[0]):
        buf_ref[c, hist:hist + rows, :] = cur[:, LANES * c:LANES * (c + 1)]
        new_tail = buf_ref[c, rows:rows + hist, :]
        from_prev = pltpu.roll(tail_ref[c].reshape(taps - 1, SUBLANES, LANES), 1, 1)
        from_cur = pltpu.roll(new_tail.reshape(taps - 1, SUBLANES, LANES), 1, 1)
        buf_ref[c, 0:hist, :] = jnp.where(sub == 0, from_prev, from_cur).reshape(hist, LANES)
        tail_ref[c] = new_tail


def _conv_weights_stage(w_ref, bias_ref, wb_ref):
    taps = w_ref.shape[0]
    for c in range(wb_ref.shape[0]):
        cs = slice(LANES * c, LANES * (c + 1))
        for k in range(taps):
            wb_ref[c, SUBLANES * k:SUBLANES * (k + 1), :] = jnp.broadcast_to(
                w_ref[k:k + 1, cs], (SUBLANES, LANES))
        wb_ref[c, SUBLANES * taps:SUBLANES * (taps + 1), :] = jnp.broadcast_to(
            bias_ref[:, cs], (SUBLANES, LANES))


def _conv_block(wb_ref, buf_ref, out_ref, taps, lane_group, row_block):
    i0 = row_block * CONV_RB
    blk = lambda ref, i: ref[lane_group, SUBLANES * i:SUBLANES * (i + 1), :]
    win = [blk(buf_ref, taps - 1 + i0 + r) for r in range(CONV_RB)]
    bias = blk(wb_ref, taps)
    acc = [bias] * CONV_RB
    for delay in range(taps):
        wv = blk(wb_ref, taps - 1 - delay)
        acc = [acc[r] + wv * win[r] for r in range(CONV_RB)]
        if delay + 1 < taps:
            win = [blk(buf_ref, taps - 1 + i0 - delay - 1)] + win[:-1]
    for r in range(CONV_RB):
        out_ref[lane_group, SUBLANES * (i0 + r):SUBLANES * (i0 + r + 1), :] = acc[r]


def _conv_lane_group(wb_ref, buf_ref, out_ref, taps, lane_group):
    for rb in range(SEG // CONV_RB):
        _conv_block(wb_ref, buf_ref, out_ref, taps, lane_group, rb)


def _sublane_affine_scan(a, b):
    row = lax.broadcasted_iota(jnp.int32, a.shape, 0)
    for k in (1, 2, 4):
        keep = row >= k
        b = jnp.where(keep, a * pltpu.roll(b, k, 0) + b, b)
        a = jnp.where(keep, a * pltpu.roll(a, k, 0), a)
    return a, b


def _mixer_body(*refs, n_col):
    _mixer_tile(refs[:n_col], *refs[n_col:])


def _mixer_tile(x_refs, mixn_ref, w_lru_ref, w_cm_ref, w_qk_ref, w_v_ref, w_og_ref,
                w_gl_ref, w_mgab_ref, w_mgc_ref,
                lcw_ref, lcb_ref, wa_ref, ba_ref, wx_ref, bx_ref, lam_ref, lwo_ref,
                ccw_ref, ccb_ref, lng_ref, lnb_ref, cwo_ref,
                wg2_ref, bg_ref, gn_ref, gwo_ref, gb_ref, wout_ref,
                o_ref,
                xs_ref, hb_ref, hbi_ref, ubuf_ref, utail_ref, uwb_ref, uout_ref,
                cbuf_ref, ctail_ref, cwb_ref, cout_ref,
                a_ref, b_ref, hs_ref, mab_ref, qk_ref, v_ref, og_ref, gc_ref, gab_ref, ya_ref,
                hst_ref, s_ref):
    rows, d = o_ref.shape
    lru_taps = lcw_ref.shape[0]
    cm_taps = ccw_ref.shape[0]
    lane_groups = d // LANES
    dk = wg2_ref.shape[1] // GLA_HEADS
    dv = gwo_ref.shape[0] // GLA_HEADS

    @pl.when(pl.program_id(1) == 0)
    def _():
        utail_ref[...] = jnp.zeros(utail_ref.shape, F32)
        ctail_ref[...] = jnp.zeros(ctail_ref.shape, F32)
        hst_ref[...] = jnp.zeros(hst_ref.shape, F32)
        s_ref[...] = jnp.zeros(s_ref.shape, F32)
        _conv_weights_stage(lcw_ref, lcb_ref, uwb_ref)
        _conv_weights_stage(ccw_ref, ccb_ref, cwb_ref)

    x = jnp.concatenate([ref[...] for ref in x_refs], axis=1)
    hb_ref[...] = _rms_scale(x, mixn_ref[...]).astype(BF16)
    hbi_ref[...] = _rms_scale(_load_interleaved(x_refs, xs_ref), mixn_ref[...]).astype(BF16)

    _conv_stage(_dot(hbi_ref[...], w_lru_ref[...]), lru_taps, ubuf_ref, utail_ref)
    val = _dot(hbi_ref[...], w_cm_ref[:, 0:d])
    for c in range(lane_groups):
        _conv_lane_group(uwb_ref, ubuf_ref, uout_ref, lru_taps, c)
    gate = _dot(hbi_ref[...], w_cm_ref[:, d:2 * d])
    u = _lane_concat(uout_ref)
    ub = u.astype(BF16)
    lam = lam_ref[...]
    cvec = -LRU_C * (jnp.maximum(-lam, 0.0) + jnp.log1p(jnp.exp(-jnp.abs(lam))))
    blk = d // LRU_HEADS
    for h in range(LRU_HEADS):
        sl = slice(h * blk, (h + 1) * blk)
        r = _sigmoid(_dot(ub[:, sl], wa_ref[h]) + ba_ref[:, sl])
        i = _sigmoid(_dot(ub[:, sl], wx_ref[h]) + bx_ref[:, sl])
        a = jnp.exp(r * cvec[:, sl])
        a_ref[:, sl] = a
        b_ref[:, sl] = jnp.sqrt(1.0 - a * a) * (i * u[:, sl])
    _conv_stage(val * _sigmoid(gate), cm_taps, cbuf_ref, ctail_ref)
    prod = lin = None
    for i in range(SEG):
        rs = slice(SUBLANES * i, SUBLANES * (i + 1))
        a = a_ref[rs, :]
        b = b_ref[rs, :]
        if i == 0:
            prod, lin = a, b
        else:
            lin = a * lin + b
            prod = a * prod
        a_ref[rs, :] = prod
        b_ref[rs, :] = lin
    h_prev = hst_ref[...]
    run_a, run_b = _sublane_affine_scan(prod, lin)
    run_end = run_a * h_prev + run_b
    row8 = lax.broadcasted_iota(jnp.int32, run_end.shape, 0)
    run_in = jnp.where(row8 == 0, h_prev, pltpu.roll(run_end, 1, 0))
    hst_ref[...] = jnp.broadcast_to(run_end[SUBLANES - 1:SUBLANES, :], run_end.shape)
    hs_ref[...] = (b_ref[...] + a_ref[...] * jnp.tile(run_in, (SEG, 1))).astype(BF16)

    def chunk_dots(j):
        return [
            lambda: qk_ref.__setitem__(j, _dot(hb_ref[...], w_qk_ref[j])),
            lambda: v_ref.__setitem__(j, _dot(hb_ref[...], w_v_ref[j]).astype(BF16)),
            lambda: og_ref.__setitem__(j, _dot(hb_ref[...], w_og_ref[j])),
            lambda: gc_ref.__setitem__(j, _dot(hb_ref[...], w_mgc_ref[j])),
            lambda: gab_ref.__setitem__(2 * j, _dot(hbi_ref[...], w_mgab_ref[2 * j])),
            lambda: gab_ref.__setitem__(2 * j + 1, _dot(hbi_ref[...], w_mgab_ref[2 * j + 1])),
            lambda: ya_ref.__setitem__(j, _dot(hs_ref[...], lwo_ref[j])),
        ]

    dots = [f for j in range(qk_ref.shape[0]) for f in chunk_dots(j)]
    always = pl.program_id(1) >= 0
    for c in range(lane_groups):
        @pl.when(always)
        def _(c=c):
            _conv_lane_group(cwb_ref, cbuf_ref, cout_ref, cm_taps, c)
            for f in dots[c::lane_groups]:
                f()

    acc = _lane_concat(cout_ref)
    mu = jnp.mean(acc, axis=-1, keepdims=True)
    xc = acc - mu
    var = jnp.mean(xc * xc, axis=-1, keepdims=True)
    ln = xc * lax.rsqrt(var + EPS) * lng_ref[...] + lnb_ref[...]
    glow = _dot(hb_ref[...], w_gl_ref[...]).astype(BF16)
    y_b = _dot((ln * _sigmoid(ln)).astype(BF16), cwo_ref[...])

    g_ab = _sigmoid(_lane_concat(gab_ref) + gb_ref[:, 0:2 * d])
    merged_i = g_ab[:, 0:d] * _lane_concat(ya_ref) + g_ab[:, d:2 * d] * y_b
    for c in range(lane_groups):
        mab_ref[c] = merged_i[:, c * LANES:(c + 1) * LANES]
    merged_ab = _load_natural(mab_ref)

    qk = _lane_concat(qk_ref)
    q = qk[:, 0:GLA_HEADS * dk]
    kk = qk[:, GLA_HEADS * dk:2 * GLA_HEADS * dk]
    z = _dot(glow, wg2_ref[...]) + bg_ref[...]
    la = (jnp.minimum(z, 0.0) - jnp.log(1.0 + jnp.exp(-jnp.abs(z)))) * (1.0 / GLA_TAU)
    ri = lax.broadcasted_iota(jnp.int32, (rows, rows), 0)
    ci = lax.broadcasted_iota(jnp.int32, (rows, rows), 1)
    causal = (ri >= ci) & ((ri // GLA_CHUNK) == (ci // GLA_CHUNK))
    tri = jnp.where(causal, 1.0, 0.0).astype(BF16)
    la_hi = la.astype(BF16)
    la_lo = (la - la_hi.astype(F32)).astype(BF16)
    bcum = _dot(tri, la_hi) + _dot(tri, la_lo)
    n_chunks = rows // GLA_CHUNK
    blast = [bcum[(c + 1) * GLA_CHUNK - 1:(c + 1) * GLA_CHUNK, :] for c in range(n_chunks)]
    blast_full = jnp.concatenate(
        [jnp.broadcast_to(bl, (GLA_CHUNK, bl.shape[1])) for bl in blast], axis=0)
    q_dec = (q * (dk ** -0.5) * jnp.exp(bcum)).astype(BF16)
    k_intra = (kk * jnp.exp(-bcum)).astype(BF16)
    k_state = (kk * jnp.exp(blast_full - bcum)).astype(BF16)

    heads = []
    for h in range(GLA_HEADS):
        ks = slice(h * dk, (h + 1) * dk)
        v_h = v_ref[h]
        sc = lax.dot_general(q_dec[:, ks], k_intra[:, ks], (((1,), (1,)), ((), ())),
                             preferred_element_type=F32)
        o_h = _dot(jnp.where(causal, sc, 0.0).astype(BF16), v_h)
        st = s_ref[h]
        inter = []
        for c in range(n_chunks):
            rs = slice(c * GLA_CHUNK, (c + 1) * GLA_CHUNK)
            inter.append(lax.dot_general(q_dec[rs, ks], st.astype(BF16),
                                         (((1,), (1,)), ((), ())), preferred_element_type=F32))
            st = st * jnp.exp(blast[c][:, ks]) + lax.dot_general(
                v_h[rs, :], k_state[rs, ks], (((0,), (0,)), ((), ())),
                preferred_element_type=F32)
        s_ref[h] = st
        o_h = o_h + jnp.concatenate(inter, axis=0)
        o_h = _rms_scale(o_h, gn_ref[...])
        og_h = og_ref[h]
        heads.append((o_h * (og_h * _sigmoid(og_h))).astype(BF16))
    y_c = _dot(jnp.concatenate(heads, axis=1), gwo_ref[...])

    g_c = _sigmoid(_lane_concat(gc_ref) + gb_ref[:, 2 * d:3 * d])
    merged = merged_ab + g_c * y_c
    o_ref[...] = x + _dot(merged.astype(BF16), wout_ref[...])


_MIXER_PARAMS = [
    "mixn", "w_lru", "w_cm", "w_qk", "w_v", "w_og", "w_gl", "w_mgab", "w_mgc",
    "lcw", "lcb", "wa", "ba", "wx", "bx", "lam", "lwo",
    "ccw", "ccb", "lng", "lnb", "cwo",
    "wg2", "bg", "gn", "gwo", "gb", "wout",
]


def _col_chunks(w):
    k, n = w.shape
    return w.reshape(k, n // MXU_COLS, MXU_COLS).transpose(1, 0, 2)


def _mixer(x2, seq_len, weights):
    n, d = x2.shape
    assert seq_len % MIX_ROWS == 0 and MIX_ROWS % GLA_CHUNK == 0 and SEG % SUBLANES == 0
    tiles = seq_len // MIX_ROWS
    batch = n // seq_len
    full = lambda a: pl.BlockSpec(a.shape, lambda b, j: (0,) * a.ndim)
    row_spec = pl.BlockSpec((MIX_ROWS, d), lambda b, j: (b * tiles + j, 0))
    n_col = d // LANES
    col_specs = [pl.BlockSpec((MIX_ROWS, LANES), lambda b, j, c=c: (b * tiles + j, c))
                 for c in range(n_col)]
    assert list(weights) == _MIXER_PARAMS
    lru_taps = weights["lcw"].shape[0]
    cm_taps = weights["ccw"].shape[0]
    assert max(lru_taps, cm_taps) <= SEG
    dk = weights["wg2"].shape[1] // GLA_HEADS
    dv = weights["gwo"].shape[0] // GLA_HEADS
    chunks = d // MXU_COLS
    assert dv == MXU_COLS and 2 * GLA_HEADS * dk == chunks * MXU_COLS
    conv_scratch = lambda taps: [
        pltpu.VMEM((n_col, MIX_ROWS + SUBLANES * (taps - 1), LANES), F32),
        pltpu.VMEM((n_col, SUBLANES * (taps - 1), LANES), F32),
        pltpu.VMEM((n_col, SUBLANES * (taps + 1), LANES), F32),
        pltpu.VMEM((n_col, MIX_ROWS, LANES), F32),
    ]
    return pl.pallas_call(
        functools.partial(_mixer_body, n_col=n_col),
        grid=(batch, tiles),
        in_specs=col_specs + [full(w) for w in weights.values()],
        out_specs=row_spec,
        out_shape=jax.ShapeDtypeStruct((n, d), F32),
        scratch_shapes=[
            pltpu.VMEM((n_col, SUBLANES * SEG_PITCH, LANES), F32),
            pltpu.VMEM((MIX_ROWS, d), BF16),
            pltpu.VMEM((MIX_ROWS, d), BF16),
            *conv_scratch(lru_taps),
            *conv_scratch(cm_taps),
            pltpu.VMEM((MIX_ROWS, d), F32),
            pltpu.VMEM((MIX_ROWS, d), F32),
            pltpu.VMEM((MIX_ROWS, d), BF16),
            pltpu.VMEM((n_col, MIX_ROWS, LANES), F32),
            pltpu.VMEM((chunks, MIX_ROWS, MXU_COLS), F32),
            pltpu.VMEM((chunks, MIX_ROWS, MXU_COLS), BF16),
            pltpu.VMEM((chunks, MIX_ROWS, MXU_COLS), F32),
            pltpu.VMEM((chunks, MIX_ROWS, MXU_COLS), F32),
            pltpu.VMEM((2 * chunks, MIX_ROWS, MXU_COLS), F32),
            pltpu.VMEM((chunks, MIX_ROWS, MXU_COLS), F32),
            pltpu.VMEM((SUBLANES, d), F32),
            pltpu.VMEM((GLA_HEADS, dv, dk), F32),
        ],
        compiler_params=pltpu.CompilerParams(
            dimension_semantics=("arbitrary", "arbitrary"), vmem_limit_bytes=VMEM_LIMIT_BYTES),
        name="mixer",
    )(*([x2] * n_col), *weights.values())


def _row(v):
    return v.reshape(1, -1).astype(F32)


def kernel(x, ffn1_norm, ffn1_w_gate, ffn1_w_up, ffn1_w_down, mix_norm, w_in, lru_conv_w, lru_conv_b, lru_w_a, lru_b_a, lru_w_x, lru_b_x, lru_lambda, lru_w_o, cm_conv_w, cm_conv_b, cm_ln_g, cm_ln_b, cm_w_o, gla_w_g2, gla_b_g, gla_norm, gla_w_o, gate_b, w_out, ffn2_norm, ffn2_w_gate, ffn2_w_up, ffn2_w_down, final_norm):
    bsz, t, d = x.shape
    depth = w_in.shape[0]
    lru_w = lru_w_o.shape[1]
    cm_w = cm_w_o.shape[1]
    qk_w = gla_w_g2.shape[2]
    v_w = gla_w_o.shape[1]
    rank = gla_w_g2.shape[1]
    widths = (lru_w, 2 * cm_w, qk_w, qk_w, v_w, v_w, rank, gate_b.shape[1])
    offs = [0]
    for w in widths:
        offs.append(offs[-1] + w)
    assert offs[-1] == w_in.shape[2]
    rank_pad = -(-rank // LANES) * LANES

    x2 = x.reshape(bsz * t, d)
    fin = _row(final_norm)
    for i in range(depth):
        x2 = _ffn(x2, _row(ffn1_norm[i]), ffn1_w_gate[i].astype(BF16), ffn1_w_up[i].astype(BF16),
                  ffn1_w_down[i].astype(BF16), fin, final=False)
        wi = w_in[i].astype(BF16)
        seg = [wi[:, offs[s]:offs[s + 1]] for s in range(len(widths))]
        w_gl = jnp.pad(seg[6], ((0, 0), (0, rank_pad - rank)))
        w_g2 = jnp.pad(gla_w_g2[i].astype(BF16), ((0, rank_pad - rank), (0, 0)))
        weights = dict(
            mixn=_row(mix_norm[i]), w_lru=seg[0], w_cm=seg[1],
            w_qk=_col_chunks(wi[:, offs[2]:offs[4]]), w_v=_col_chunks(seg[4]),
            w_og=_col_chunks(seg[5]), w_gl=w_gl,
            w_mgab=_col_chunks(seg[7][:, 0:2 * d]), w_mgc=_col_chunks(seg[7][:, 2 * d:3 * d]),
            lcw=lru_conv_w[i], lcb=_row(lru_conv_b[i]), wa=lru_w_a[i].astype(BF16),
            ba=_row(lru_b_a[i]), wx=lru_w_x[i].astype(BF16), bx=_row(lru_b_x[i]),
            lam=_row(lru_lambda[i]), lwo=_col_chunks(lru_w_o[i].astype(BF16)),
            ccw=cm_conv_w[i], ccb=_row(cm_conv_b[i]), lng=_row(cm_ln_g[i]), lnb=_row(cm_ln_b[i]),
            cwo=cm_w_o[i].astype(BF16),
            wg2=w_g2, bg=_row(gla_b_g[i]), gn=_row(gla_norm[i]), gwo=gla_w_o[i].astype(BF16),
            gb=_row(gate_b[i]), wout=w_out[i].astype(BF16),
        )
        x2 = _mixer(x2, t, weights)
        x2 = _ffn(x2, _row(ffn2_norm[i]), ffn2_w_gate[i].astype(BF16), ffn2_w_up[i].astype(BF16),
                  ffn2_w_down[i].astype(BF16), fin, final=(i == depth - 1))
    return x2.reshape(bsz, t, d)
```

```python
import functools

import jax
import jax.numpy as jnp
from jax import lax
from jax.experimental import pallas as pl
from jax.experimental.pallas import tpu as pltpu

F32 = jnp.float32
BF16 = jnp.bfloat16

EPS = 1e-6
LRU_HEADS = 4
LRU_C = 8.0
GLA_HEADS = 4
GLA_TAU = 16.0
GLA_CHUNK = 64

LANES = 128
SUBLANES = 8
MXU_COLS = 256
VMEM_LIMIT_BYTES = 56 * 1024 * 1024

FFN_ROWS = 1024
MIX_ROWS = 256
SEG = MIX_ROWS // SUBLANES
SEG_PITCH = SEG + SUBLANES
CONV_RB = 8


def _dot(a, b):
    return jnp.dot(a, b, preferred_element_type=F32)


def _sigmoid(x):
    return jax.nn.sigmoid(x)


def _rms_scale(x, g):
    ms = jnp.mean(x * x, axis=-1, keepdims=True)
    return x * lax.rsqrt(ms + EPS) * g


def _ffn_body(x_ref, g_ref, wg_ref, wu_ref, wd_ref, fin_ref, o_ref, act_ref, *, final):
    x = x_ref[...]
    h = _rms_scale(x, g_ref[...]).astype(BF16)
    d_ff = wg_ref.shape[1]
    for c in range(d_ff // MXU_COLS):
        sl = slice(c * MXU_COLS, (c + 1) * MXU_COLS)
        gate = _dot(h, wg_ref[:, sl])
        up = _dot(h, wu_ref[:, sl])
        act_ref[:, sl] = (gate * _sigmoid(gate) * up).astype(BF16)
    y = x + 0.5 * _dot(act_ref[...], wd_ref[...])
    if final:
        y = _rms_scale(y, fin_ref[...])
    o_ref[...] = y


def _ffn(x2, g, wg, wu, wd, fin, *, final):
    n, d = x2.shape
    d_ff = wg.shape[1]
    assert n % FFN_ROWS == 0 and d_ff % MXU_COLS == 0
    full = lambda a: pl.BlockSpec(a.shape, lambda i: (0,) * a.ndim)
    return pl.pallas_call(
        functools.partial(_ffn_body, final=final),
        grid=(n // FFN_ROWS,),
        in_specs=[pl.BlockSpec((FFN_ROWS, d), lambda i: (i, 0)),
                  full(g), full(wg), full(wu), full(wd), full(fin)],
        out_specs=pl.BlockSpec((FFN_ROWS, d), lambda i: (i, 0)),
        out_shape=jax.ShapeDtypeStruct((n, d), F32),
        scratch_shapes=[pltpu.VMEM((FFN_ROWS, d_ff), BF16)],
        compiler_params=pltpu.CompilerParams(
            dimension_semantics=("arbitrary",), vmem_limit_bytes=VMEM_LIMIT_BYTES),
        name="ffn_final" if final else "ffn",
    )(x2, g, wg, wu, wd, fin)


def _load_interleaved(col_refs, xs_ref):
    for c, ref in enumerate(col_refs):
        for j in range(SUBLANES):
            xs_ref[c, SEG_PITCH * j:SEG_PITCH * j + SEG, :] = ref[SEG * j:SEG * (j + 1), :]
    return jnp.concatenate(
        [jnp.concatenate([xs_ref[c, pl.ds(i, SUBLANES, stride=SEG_PITCH), :]
                          for c in range(len(col_refs))], axis=1)
         for i in range(SEG)], axis=0)


def _load_natural(ref):
    per = SEG // SUBLANES
    return jnp.concatenate(
        [jnp.concatenate(
            [ref[c, pl.ds(SUBLANES * SUBLANES * (r % per) + r // per, SUBLANES, stride=SUBLANES), :]
             for c in range(ref.shape[0])], axis=1)
         for r in range(SEG)], axis=0)


def _lane_concat(ref):
    return jnp.concatenate([ref[c] for c in range(ref.shape[0])], axis=1)


def _conv_stage(cur, taps, buf_ref, tail_ref):
    rows = cur.shape[0]
    hist = SUBLANES * (taps - 1)
    sub = lax.broadcasted_iota(jnp.int32, (taps - 1, SUBLANES, LANES), 1)
    for c in range(buf_ref.shape[0]):
        buf_ref[c, hist:hist + rows, :] = cur[:, LANES * c:LANES * (c + 1)]
        new_tail = buf_ref[c, rows:rows + hist, :]
        from_prev = pltpu.roll(tail_ref[c].reshape(taps - 1, SUBLANES, LANES), 1, 1)
        from_cur = pltpu.roll(new_tail.reshape(taps - 1, SUBLANES, LANES), 1, 1)
        buf_ref[c, 0:hist, :] = jnp.where(sub == 0, from_prev, from_cur).reshape(hist, LANES)
        tail_ref[c] = new_tail


def _conv_weights_stage(w_ref, bias_ref, wb_ref):
    taps = w_ref.shape[0]
    for c in range(wb_ref.shape[0]):
        cs = slice(LANES * c, LANES * (c + 1))
        for k in range(taps):
            wb_ref[c, SUBLANES * k:SUBLANES * (k + 1), :] = jnp.broadcast_to(
                w_ref[k:k + 1, cs], (SUBLANES, LANES))
        wb_ref[c, SUBLANES * taps:SUBLANES * (taps + 1), :] = jnp.broadcast_to(
            bias_ref[:, cs], (SUBLANES, LANES))


def _conv_block(wb_ref, buf_ref, out_ref, taps, lane_group, row_block):
    i0 = row_block * CONV_RB
    blk = lambda ref, i: ref[lane_group, SUBLANES * i:SUBLANES * (i + 1), :]
    win = [blk(buf_ref, taps - 1 + i0 + r) for r in range(CONV_RB)]
    bias = blk(wb_ref, taps)
    acc = [bias] * CONV_RB
    for delay in range(taps):
        wv = blk(wb_ref, taps - 1 - delay)
        acc = [acc[r] + wv * win[r] for r in range(CONV_RB)]
        if delay + 1 < taps:
            win = [blk(buf_ref, taps - 1 + i0 - delay - 1)] + win[:-1]
    for r in range(CONV_RB):
        out_ref[lane_group, SUBLANES * (i0 + r):SUBLANES * (i0 + r + 1), :] = acc[r]


def _conv_lane_group(wb_ref, buf_ref, out_ref, taps, lane_group):
    for rb in range(SEG // CONV_RB):
        _conv_block(wb_ref, buf_ref, out_ref, taps, lane_group, rb)


def _sublane_affine_scan(a, b):
    row = lax.broadcasted_iota(jnp.int32, a.shape, 0)
    for k in (1, 2, 4):
        keep = row >= k
        b = jnp.where(keep, a * pltpu.roll(b, k, 0) + b, b)
        a = jnp.where(keep, a * pltpu.roll(a, k, 0), a)
    return a, b


def _mixer_body(*refs, n_col):
    _mixer_tile(refs[:n_col], *refs[n_col:])


def _mixer_tile(x_refs, mixn_ref, w_lru_ref, w_cm_ref, w_qk_ref, w_v_ref, w_og_ref,
                w_gl_ref, w_mgab_ref, w_mgc_ref,
                lcw_ref, lcb_ref, wa_ref, ba_ref, wx_ref, bx_ref, lam_ref, lwo_ref,
                ccw_ref, ccb_ref, lng_ref, lnb_ref, cwo_ref,
                wg2_ref, bg_ref, gn_ref, gwo_ref, gb_ref, wout_ref,
                o_ref,
                xs_ref, hb_ref, hbi_ref, ubuf_ref, utail_ref, uwb_ref, uout_ref,
                cbuf_ref, ctail_ref, cwb_ref, cout_ref,
                a_ref, b_ref, hs_ref, mab_ref, qk_ref, v_ref, og_ref, gc_ref, gab_ref, ya_ref,
                hst_ref, s_ref):
    rows, d = o_ref.shape
    lru_taps = lcw_ref.shape[0]
    cm_taps = ccw_ref.shape[0]
    lane_groups = d // LANES
    dk = wg2_ref.shape[1] // GLA_HEADS
    dv = gwo_ref.shape[0] // GLA_HEADS

    @pl.when(pl.program_id(1) == 0)
    def _():
        utail_ref[...] = jnp.zeros(utail_ref.shape, F32)
        ctail_ref[...] = jnp.zeros(ctail_ref.shape, F32)
        hst_ref[...] = jnp.zeros(hst_ref.shape, F32)
        s_ref[...] = jnp.zeros(s_ref.shape, F32)
        _conv_weights_stage(lcw_ref, lcb_ref, uwb_ref)
        _conv_weights_stage(ccw_ref, ccb_ref, cwb_ref)

    x = jnp.concatenate([ref[...] for ref in x_refs], axis=1)
    hb_ref[...] = _rms_scale(x, mixn_ref[...]).astype(BF16)
    hbi_ref[...] = _rms_scale(_load_interleaved(x_refs, xs_ref), mixn_ref[...]).astype(BF16)

    _conv_stage(_dot(hbi_ref[...], w_lru_ref[...]), lru_taps, ubuf_ref, utail_ref)
    val = _dot(hbi_ref[...], w_cm_ref[:, 0:d])
    for c in range(lane_groups):
        _conv_lane_group(uwb_ref, ubuf_ref, uout_ref, lru_taps, c)
    gate = _dot(hbi_ref[...], w_cm_ref[:, d:2 * d])
    u = _lane_concat(uout_ref)
    ub = u.astype(BF16)
    lam = lam_ref[...]
    cvec = -LRU_C * (jnp.maximum(-lam, 0.0) + jnp.log1p(jnp.exp(-jnp.abs(lam))))
    blk = d // LRU_HEADS
    for h in range(LRU_HEADS):
        sl = slice(h * blk, (h + 1) * blk)
        r = _sigmoid(_dot(ub[:, sl], wa_ref[h]) + ba_ref[:, sl])
        i = _sigmoid(_dot(ub[:, sl], wx_ref[h]) + bx_ref[:, sl])
        a = jnp.exp(r * cvec[:, sl])
        a_ref[:, sl] = a
        b_ref[:, sl] = jnp.sqrt(1.0 - a * a) * (i * u[:, sl])
        qk_ref[h] = _dot(hb_ref[...], w_qk_ref[h])
        v_ref[h] = _dot(hb_ref[...], w_v_ref[h]).astype(BF16)
        og_ref[h] = _dot(hb_ref[...], w_og_ref[h])
        gc_ref[h] = _dot(hb_ref[...], w_mgc_ref[h])
        gab_ref[2 * h] = _dot(hbi_ref[...], w_mgab_ref[2 * h])
        gab_ref[2 * h + 1] = _dot(hbi_ref[...], w_mgab_ref[2 * h + 1])
    _conv_stage(val * _sigmoid(gate), cm_taps, cbuf_ref, ctail_ref)
    prod = lin = None
    for i in range(SEG):
        rs = slice(SUBLANES * i, SUBLANES * (i + 1))
        a = a_ref[rs, :]
        b = b_ref[rs, :]
        if i == 0:
            prod, lin = a, b
        else:
            lin = a * lin + b
            prod = a * prod
        a_ref[rs, :] = prod
        b_ref[rs, :] = lin
    h_prev = hst_ref[...]
    run_a, run_b = _sublane_affine_scan(prod, lin)
    run_end = run_a * h_prev + run_b
    row8 = lax.broadcasted_iota(jnp.int32, run_end.shape, 0)
    run_in = jnp.where(row8 == 0, h_prev, pltpu.roll(run_end, 1, 0))
    hst_ref[...] = jnp.broadcast_to(run_end[SUBLANES - 1:SUBLANES, :], run_end.shape)
    hs_ref[...] = (b_ref[...] + a_ref[...] * jnp.tile(run_in, (SEG, 1))).astype(BF16)

    def conv_step(c, carry):
        _conv_lane_group(cwb_ref, cbuf_ref, cout_ref, cm_taps, c)
        return carry

    lax.fori_loop(0, lane_groups, conv_step, 0)
    for j in range(ya_ref.shape[0]):
        ya_ref[j] = _dot(hs_ref[...], lwo_ref[j])

    acc = _lane_concat(cout_ref)
    mu = jnp.mean(acc, axis=-1, keepdims=True)
    xc = acc - mu
    var = jnp.mean(xc * xc, axis=-1, keepdims=True)
    ln = xc * lax.rsqrt(var + EPS) * lng_ref[...] + lnb_ref[...]
    glow = _dot(hb_ref[...], w_gl_ref[...]).astype(BF16)
    y_b = _dot((ln * _sigmoid(ln)).astype(BF16), cwo_ref[...])

    g_ab = _sigmoid(_lane_concat(gab_ref) + gb_ref[:, 0:2 * d])
    merged_i = g_ab[:, 0:d] * _lane_concat(ya_ref) + g_ab[:, d:2 * d] * y_b
    for c in range(lane_groups):
        mab_ref[c] = merged_i[:, c * LANES:(c + 1) * LANES]
    merged_ab = _load_natural(mab_ref)

    qk = _lane_concat(qk_ref)
    q = qk[:, 0:GLA_HEADS * dk]
    kk = qk[:, GLA_HEADS * dk:2 * GLA_HEADS * dk]
    z = _dot(glow, wg2_ref[...]) + bg_ref[...]
    la = (jnp.minimum(z, 0.0) - jnp.log(1.0 + jnp.exp(-jnp.abs(z)))) * (1.0 / GLA_TAU)
    ri = lax.broadcasted_iota(jnp.int32, (rows, rows), 0)
    ci = lax.broadcasted_iota(jnp.int32, (rows, rows), 1)
    causal = (ri >= ci) & ((ri // GLA_CHUNK) == (ci // GLA_CHUNK))
    tri = jnp.where(causal, 1.0, 0.0).astype(BF16)
    la_hi = la.astype(BF16)
    la_lo = (la - la_hi.astype(F32)).astype(BF16)
    bcum = _dot(tri, la_hi) + _dot(tri, la_lo)
    n_chunks = rows // GLA_CHUNK
    blast = [bcum[(c + 1) * GLA_CHUNK - 1:(c + 1) * GLA_CHUNK, :] for c in range(n_chunks)]
    blast_full = jnp.concatenate(
        [jnp.broadcast_to(bl, (GLA_CHUNK, bl.shape[1])) for bl in blast], axis=0)
    q_dec = (q * (dk ** -0.5) * jnp.exp(bcum)).astype(BF16)
    k_intra = (kk * jnp.exp(-bcum)).astype(BF16)
    k_state = (kk * jnp.exp(blast_full - bcum)).astype(BF16)

    heads = []
    for h in range(GLA_HEADS):
        ks = slice(h * dk, (h + 1) * dk)
        v_h = v_ref[h]
        sc = lax.dot_general(q_dec[:, ks], k_intra[:, ks], (((1,), (1,)), ((), ())),
                             preferred_element_type=F32)
        o_h = _dot(jnp.where(causal, sc, 0.0).astype(BF16), v_h)
        st = s_ref[h]
        inter = []
        for c in range(n_chunks):
            rs = slice(c * GLA_CHUNK, (c + 1) * GLA_CHUNK)
            inter.append(lax.dot_general(q_dec[rs, ks], st.astype(BF16),
                                         (((1,), (1,)), ((), ())), preferred_element_type=F32))
            st = st * jnp.exp(blast[c][:, ks]) + lax.dot_general(
                v_h[rs, :], k_state[rs, ks], (((0,), (0,)), ((), ())),
                preferred_element_type=F32)
        s_ref[h] = st
        o_h = o_h + jnp.concatenate(inter, axis=0)
        o_h = _rms_scale(o_h, gn_ref[...])
        og_h = og_ref[h]
        heads.append((o_h * (og_h * _sigmoid(og_h))).astype(BF16))
    y_c = _dot(jnp.concatenate(heads, axis=1), gwo_ref[...])

    g_c = _sigmoid(_lane_concat(gc_ref) + gb_ref[:, 2 * d:3 * d])
    merged = merged_ab + g_c * y_c
    o_ref[...] = x + _dot(merged.astype(BF16), wout_ref[...])


_MIXER_PARAMS = [
    "mixn", "w_lru", "w_cm", "w_qk", "w_v", "w_og", "w_gl", "w_mgab", "w_mgc",
    "lcw", "lcb", "wa", "ba", "wx", "bx", "lam", "lwo",
    "ccw", "ccb", "lng", "lnb", "cwo",
    "wg2", "bg", "gn", "gwo", "gb", "wout",
]


def _col_chunks(w):
    k, n = w.shape
    return w.reshape(k, n // MXU_COLS, MXU_COLS).transpose(1, 0, 2)


def _mixer(x2, seq_len, weights):
    n, d = x2.shape
    assert seq_len % MIX_ROWS == 0 and MIX_ROWS % GLA_CHUNK == 0 and SEG % SUBLANES == 0
    tiles = seq_len // MIX_ROWS
    batch = n // seq_len
    full = lambda a: pl.BlockSpec(a.shape, lambda b, j: (0,) * a.ndim)
    row_spec = pl.BlockSpec((MIX_ROWS, d), lambda b, j: (b * tiles + j, 0))
    n_col = d // LANES
    col_specs = [pl.BlockSpec((MIX_ROWS, LANES), lambda b, j, c=c: (b * tiles + j, c))
                 for c in range(n_col)]
    assert list(weights) == _MIXER_PARAMS
    lru_taps = weights["lcw"].shape[0]
    cm_taps = weights["ccw"].shape[0]
    assert max(lru_taps, cm_taps) <= SEG
    dk = weights["wg2"].shape[1] // GLA_HEADS
    dv = weights["gwo"].shape[0] // GLA_HEADS
    chunks = d // MXU_COLS
    assert dv == MXU_COLS and 2 * GLA_HEADS * dk == chunks * MXU_COLS and chunks == LRU_HEADS
    conv_scratch = lambda taps: [
        pltpu.VMEM((n_col, MIX_ROWS + SUBLANES * (taps - 1), LANES), F32),
        pltpu.VMEM((n_col, SUBLANES * (taps - 1), LANES), F32),
        pltpu.VMEM((n_col, SUBLANES * (taps + 1), LANES), F32),
        pltpu.VMEM((n_col, MIX_ROWS, LANES), F32),
    ]
    return pl.pallas_call(
        functools.partial(_mixer_body, n_col=n_col),
        grid=(batch, tiles),
        in_specs=col_specs + [full(w) for w in weights.values()],
        out_specs=row_spec,
        out_shape=jax.ShapeDtypeStruct((n, d), F32),
        scratch_shapes=[
            pltpu.VMEM((n_col, SUBLANES * SEG_PITCH, LANES), F32),
            pltpu.VMEM((MIX_ROWS, d), BF16),
            pltpu.VMEM((MIX_ROWS, d), BF16),
            *conv_scratch(lru_taps),
            *conv_scratch(cm_taps),
            pltpu.VMEM((MIX_ROWS, d), F32),
            pltpu.VMEM((MIX_ROWS, d), F32),
            pltpu.VMEM((MIX_ROWS, d), BF16),
            pltpu.VMEM((n_col, MIX_ROWS, LANES), F32),
            pltpu.VMEM((chunks, MIX_ROWS, MXU_COLS), F32),
            pltpu.VMEM((chunks, MIX_ROWS, MXU_COLS), BF16),
            pltpu.VMEM((chunks, MIX_ROWS, MXU_COLS), F32),
            pltpu.VMEM((chunks, MIX_ROWS, MXU_COLS), F32),
            pltpu.VMEM((2 * chunks, MIX_ROWS, MXU_COLS), F32),
            pltpu.VMEM((chunks, MIX_ROWS, MXU_COLS), F32),
            pltpu.VMEM((SUBLANES, d), F32),
            pltpu.VMEM((GLA_HEADS, dv, dk), F32),
        ],
        compiler_params=pltpu.CompilerParams(
            dimension_semantics=("arbitrary", "arbitrary"), vmem_limit_bytes=VMEM_LIMIT_BYTES),
        name="mixer",
    )(*([x2] * n_col), *weights.values())


def _row(v):
    return v.reshape(1, -1).astype(F32)


def kernel(x, ffn1_norm, ffn1_w_gate, ffn1_w_up, ffn1_w_down, mix_norm, w_in, lru_conv_w, lru_conv_b, lru_w_a, lru_b_a, lru_w_x, lru_b_x, lru_lambda, lru_w_o, cm_conv_w, cm_conv_b, cm_ln_g, cm_ln_b, cm_w_o, gla_w_g2, gla_b_g, gla_norm, gla_w_o, gate_b, w_out, ffn2_norm, ffn2_w_gate, ffn2_w_up, ffn2_w_down, final_norm):
    bsz, t, d = x.shape
    depth = w_in.shape[0]
    lru_w = lru_w_o.shape[1]
    cm_w = cm_w_o.shape[1]
    qk_w = gla_w_g2.shape[2]
    v_w = gla_w_o.shape[1]
    rank = gla_w_g2.shape[1]
    widths = (lru_w, 2 * cm_w, qk_w, qk_w, v_w, v_w, rank, gate_b.shape[1])
    offs = [0]
    for w in widths:
        offs.append(offs[-1] + w)
    assert offs[-1] == w_in.shape[2]
    rank_pad = -(-rank // LANES) * LANES

    x2 = x.reshape(bsz * t, d)
    fin = _row(final_norm)
    for i in range(depth):
        x2 = _ffn(x2, _row(ffn1_norm[i]), ffn1_w_gate[i].astype(BF16), ffn1_w_up[i].astype(BF16),
                  ffn1_w_down[i].astype(BF16), fin, final=False)
        wi = w_in[i].astype(BF16)
        seg = [wi[:, offs[s]:offs[s + 1]] for s in range(len(widths))]
        w_gl = jnp.pad(seg[6], ((0, 0), (0, rank_pad - rank)))
        w_g2 = jnp.pad(gla_w_g2[i].astype(BF16), ((0, rank_pad - rank), (0, 0)))
        weights = dict(
            mixn=_row(mix_norm[i]), w_lru=seg[0], w_cm=seg[1],
            w_qk=_col_chunks(wi[:, offs[2]:offs[4]]), w_v=_col_chunks(seg[4]),
            w_og=_col_chunks(seg[5]), w_gl=w_gl,
            w_mgab=_col_chunks(seg[7][:, 0:2 * d]), w_mgc=_col_chunks(seg[7][:, 2 * d:3 * d]),
            lcw=lru_conv_w[i], lcb=_row(lru_conv_b[i]), wa=lru_w_a[i].astype(BF16),
            ba=_row(lru_b_a[i]), wx=lru_w_x[i].astype(BF16), bx=_row(lru_b_x[i]),
            lam=_row(lru_lambda[i]), lwo=_col_chunks(lru_w_o[i].astype(BF16)),
            ccw=cm_conv_w[i], ccb=_row(cm_conv_b[i]), lng=_row(cm_ln_g[i]), lnb=_row(cm_ln_b[i]),
            cwo=cm_w_o[i].astype(BF16),
            wg2=w_g2, bg=_row(gla_b_g[i]), gn=_row(gla_norm[i]), gwo=gla_w_o[i].astype(BF16),
            gb=_row(gate_b[i]), wout=w_out[i].astype(BF16),
        )
        x2 = _mixer(x2, t, weights)
        x2 = _ffn(x2, _row(ffn2_norm[i]), ffn2_w_gate[i].astype(BF16), ffn2_w_up[i].astype(BF16),
                  ffn2_w_down[i].astype(BF16), fin, final=(i == depth - 1))
    return x2.reshape(bsz, t, d)
```

```python
import functools

import jax
import jax.numpy as jnp
from jax import lax
from jax.experimental import pallas as pl
from jax.experimental.pallas import tpu as pltpu

F32 = jnp.float32
BF16 = jnp.bfloat16

EPS = 1e-6
LRU_HEADS = 4
LRU_C = 8.0
GLA_HEADS = 4
GLA_TAU = 16.0
GLA_CHUNK = 64

LANES = 128
SUBLANES = 8
MXU_COLS = 256
VMEM_LIMIT_BYTES = 56 * 1024 * 1024

FFN_ROWS = 1024
MIX_ROWS = 256
SEG = MIX_ROWS // SUBLANES
CONV_RB = 8


def _dot(a, b):
    return jnp.dot(a, b, preferred_element_type=F32)


def _sigmoid(x):
    return jax.nn.sigmoid(x)


def _rms_scale(x, g):
    ms = jnp.mean(x * x, axis=-1, keepdims=True)
    return x * lax.rsqrt(ms + EPS) * g


def _ffn_body(x_ref, g_ref, wg_ref, wu_ref, wd_ref, fin_ref, o_ref, act_ref, *, final):
    x = x_ref[...]
    h = _rms_scale(x, g_ref[...]).astype(BF16)
    d_ff = wg_ref.shape[1]
    for c in range(d_ff // MXU_COLS):
        sl = slice(c * MXU_COLS, (c + 1) * MXU_COLS)
        gate = _dot(h, wg_ref[:, sl])
        up = _dot(h, wu_ref[:, sl])
        act_ref[:, sl] = (gate * _sigmoid(gate) * up).astype(BF16)
    y = x + 0.5 * _dot(act_ref[...], wd_ref[...])
    if final:
        y = _rms_scale(y, fin_ref[...])
    o_ref[...] = y


def _ffn(x2, g, wg, wu, wd, fin, *, final):
    n, d = x2.shape
    d_ff = wg.shape[1]
    assert n % FFN_ROWS == 0 and d_ff % MXU_COLS == 0
    full = lambda a: pl.BlockSpec(a.shape, lambda i: (0,) * a.ndim)
    return pl.pallas_call(
        functools.partial(_ffn_body, final=final),
        grid=(n // FFN_ROWS,),
        in_specs=[pl.BlockSpec((FFN_ROWS, d), lambda i: (i, 0)),
                  full(g), full(wg), full(wu), full(wd), full(fin)],
        out_specs=pl.BlockSpec((FFN_ROWS, d), lambda i: (i, 0)),
        out_shape=jax.ShapeDtypeStruct((n, d), F32),
        scratch_shapes=[pltpu.VMEM((FFN_ROWS, d_ff), BF16)],
        compiler_params=pltpu.CompilerParams(
            dimension_semantics=("arbitrary",), vmem_limit_bytes=VMEM_LIMIT_BYTES),
        name="ffn_final" if final else "ffn",
    )(x2, g, wg, wu, wd, fin)


def _interleave_matrix(rows):
    p = lax.broadcasted_iota(jnp.int32, (rows, rows), 0)
    t = lax.broadcasted_iota(jnp.int32, (rows, rows), 1)
    return jnp.where(t == SEG * (p % SUBLANES) + p // SUBLANES, 1.0, 0.0).astype(BF16)


def _load_natural(ref):
    per = SEG // SUBLANES
    return jnp.concatenate(
        [jnp.concatenate(
            [ref[c, pl.ds(SUBLANES * SUBLANES * (r % per) + r // per, SUBLANES, stride=SUBLANES), :]
             for c in range(ref.shape[0])], axis=1)
         for r in range(SEG)], axis=0)


def _lane_concat(ref):
    return jnp.concatenate([ref[c] for c in range(ref.shape[0])], axis=1)


def _conv_stage(cur, taps, buf_ref, tail_ref):
    rows = cur.shape[0]
    hist = SUBLANES * (taps - 1)
    sub = lax.broadcasted_iota(jnp.int32, (taps - 1, SUBLANES, LANES), 1)
    for c in range(buf_ref.shape[0]):
        buf_ref[c, hist:hist + rows, :] = cur[:, LANES * c:LANES * (c + 1)]
        new_tail = buf_ref[c, rows:rows + hist, :]
        from_prev = pltpu.roll(tail_ref[c].reshape(taps - 1, SUBLANES, LANES), 1, 1)
        from_cur = pltpu.roll(new_tail.reshape(taps - 1, SUBLANES, LANES), 1, 1)
        buf_ref[c, 0:hist, :] = jnp.where(sub == 0, from_prev, from_cur).reshape(hist, LANES)
        tail_ref[c] = new_tail


def _conv_weights_stage(w_ref, bias_ref, wb_ref):
    taps = w_ref.shape[0]
    for c in range(wb_ref.shape[0]):
        cs = slice(LANES * c, LANES * (c + 1))
        for k in range(taps):
            wb_ref[c, SUBLANES * k:SUBLANES * (k + 1), :] = jnp.broadcast_to(
                w_ref[k:k + 1, cs], (SUBLANES, LANES))
        wb_ref[c, SUBLANES * taps:SUBLANES * (taps + 1), :] = jnp.broadcast_to(
            bias_ref[:, cs], (SUBLANES, LANES))


def _conv_block(wb_ref, buf_ref, out_ref, taps, lane_group, row_block):
    i0 = row_block * CONV_RB
    blk = lambda ref, i: ref[lane_group, SUBLANES * i:SUBLANES * (i + 1), :]
    win = [blk(buf_ref, taps - 1 + i0 + r) for r in range(CONV_RB)]
    bias = blk(wb_ref, taps)
    acc = [bias] * CONV_RB
    for delay in range(taps):
        wv = blk(wb_ref, taps - 1 - delay)
        acc = [acc[r] + wv * win[r] for r in range(CONV_RB)]
        if delay + 1 < taps:
            win = [blk(buf_ref, taps - 1 + i0 - delay - 1)] + win[:-1]
    for r in range(CONV_RB):
        out_ref[lane_group, SUBLANES * (i0 + r):SUBLANES * (i0 + r + 1), :] = acc[r]


def _conv_lane_group(wb_ref, buf_ref, out_ref, taps, lane_group):
    for rb in range(SEG // CONV_RB):
        _conv_block(wb_ref, buf_ref, out_ref, taps, lane_group, rb)


def _sublane_affine_scan(a, b):
    row = lax.broadcasted_iota(jnp.int32, a.shape, 0)
    for k in (1, 2, 4):
        keep = row >= k
        b = jnp.where(keep, a * pltpu.roll(b, k, 0) + b, b)
        a = jnp.where(keep, a * pltpu.roll(a, k, 0), a)
    return a, b


def _mixer_body(x_ref, mixn_ref, w_lru_ref, w_cm_ref, w_qk_ref, w_v_ref, w_og_ref,
                w_gl_ref, w_mgab_ref, w_mgc_ref,
                lcw_ref, lcb_ref, wa_ref, ba_ref, wx_ref, bx_ref, lam_ref, lwo_ref,
                ccw_ref, ccb_ref, lng_ref, lnb_ref, cwo_ref,
                wg2_ref, bg_ref, gn_ref, gwo_ref, gb_ref, wout_ref,
                o_ref,
                perm_ref, hb_ref, hbi_ref, ubuf_ref, utail_ref, uwb_ref, uout_ref,
                cbuf_ref, ctail_ref, cwb_ref, cout_ref,
                a_ref, b_ref, hs_ref, mab_ref, qk_ref, v_ref, og_ref, gc_ref, gab_ref, ya_ref,
                hst_ref, s_ref):
    rows, d = o_ref.shape
    lru_taps = lcw_ref.shape[0]
    cm_taps = ccw_ref.shape[0]
    lane_groups = d // LANES
    dk = wg2_ref.shape[1] // GLA_HEADS
    dv = gwo_ref.shape[0] // GLA_HEADS

    @pl.when(pl.program_id(1) == 0)
    def _():
        utail_ref[...] = jnp.zeros(utail_ref.shape, F32)
        ctail_ref[...] = jnp.zeros(ctail_ref.shape, F32)
        hst_ref[...] = jnp.zeros(hst_ref.shape, F32)
        s_ref[...] = jnp.zeros(s_ref.shape, F32)
        _conv_weights_stage(lcw_ref, lcb_ref, uwb_ref)
        _conv_weights_stage(ccw_ref, ccb_ref, cwb_ref)
        perm_ref[...] = _interleave_matrix(rows)

    x = x_ref[...]
    hb_ref[...] = _rms_scale(x, mixn_ref[...]).astype(BF16)
    hbi_ref[...] = _dot(perm_ref[...], hb_ref[...]).astype(BF16)

    _conv_stage(_dot(hbi_ref[...], w_lru_ref[...]), lru_taps, ubuf_ref, utail_ref)
    val = _dot(hbi_ref[...], w_cm_ref[:, 0:d])
    for c in range(lane_groups):
        _conv_lane_group(uwb_ref, ubuf_ref, uout_ref, lru_taps, c)
    gate = _dot(hbi_ref[...], w_cm_ref[:, d:2 * d])
    u = _lane_concat(uout_ref)
    ub = u.astype(BF16)
    lam = lam_ref[...]
    cvec = -LRU_C * (jnp.maximum(-lam, 0.0) + jnp.log1p(jnp.exp(-jnp.abs(lam))))
    blk = d // LRU_HEADS
    for h in range(LRU_HEADS):
        sl = slice(h * blk, (h + 1) * blk)
        r = _sigmoid(_dot(ub[:, sl], wa_ref[h]) + ba_ref[:, sl])
        i = _sigmoid(_dot(ub[:, sl], wx_ref[h]) + bx_ref[:, sl])
        a = jnp.exp(r * cvec[:, sl])
        a_ref[:, sl] = a
        b_ref[:, sl] = jnp.sqrt(1.0 - a * a) * (i * u[:, sl])
    _conv_stage(val * _sigmoid(gate), cm_taps, cbuf_ref, ctail_ref)
    prod = lin = None
    for i in range(SEG):
        rs = slice(SUBLANES * i, SUBLANES * (i + 1))
        a = a_ref[rs, :]
        b = b_ref[rs, :]
        if i == 0:
            prod, lin = a, b
        else:
            lin = a * lin + b
            prod = a * prod
        a_ref[rs, :] = prod
        b_ref[rs, :] = lin
    h_prev = hst_ref[...]
    run_a, run_b = _sublane_affine_scan(prod, lin)
    run_end = run_a * h_prev + run_b
    row8 = lax.broadcasted_iota(jnp.int32, run_end.shape, 0)
    run_in = jnp.where(row8 == 0, h_prev, pltpu.roll(run_end, 1, 0))
    hst_ref[...] = jnp.broadcast_to(run_end[SUBLANES - 1:SUBLANES, :], run_end.shape)
    hs_ref[...] = (b_ref[...] + a_ref[...] * jnp.tile(run_in, (SEG, 1))).astype(BF16)

    def conv_step(c, carry):
        _conv_lane_group(cwb_ref, cbuf_ref, cout_ref, cm_taps, c)
        return carry

    lax.fori_loop(0, lane_groups, conv_step, 0)
    for j in range(qk_ref.shape[0]):
        qk_ref[j] = _dot(hb_ref[...], w_qk_ref[j])
        v_ref[j] = _dot(hb_ref[...], w_v_ref[j]).astype(BF16)
        og_ref[j] = _dot(hb_ref[...], w_og_ref[j])
        gc_ref[j] = _dot(hb_ref[...], w_mgc_ref[j])
        gab_ref[2 * j] = _dot(hbi_ref[...], w_mgab_ref[2 * j])
        gab_ref[2 * j + 1] = _dot(hbi_ref[...], w_mgab_ref[2 * j + 1])
        ya_ref[j] = _dot(hs_ref[...], lwo_ref[j])

    acc = _lane_concat(cout_ref)
    mu = jnp.mean(acc, axis=-1, keepdims=True)
    xc = acc - mu
    var = jnp.mean(xc * xc, axis=-1, keepdims=True)
    ln = xc * lax.rsqrt(var + EPS) * lng_ref[...] + lnb_ref[...]
    glow = _dot(hb_ref[...], w_gl_ref[...]).astype(BF16)
    y_b = _dot((ln * _sigmoid(ln)).astype(BF16), cwo_ref[...])

    g_ab = _sigmoid(_lane_concat(gab_ref) + gb_ref[:, 0:2 * d])
    merged_i = g_ab[:, 0:d] * _lane_concat(ya_ref) + g_ab[:, d:2 * d] * y_b
    for c in range(lane_groups):
        mab_ref[c] = merged_i[:, c * LANES:(c + 1) * LANES]
    merged_ab = _load_natural(mab_ref)

    qk = _lane_concat(qk_ref)
    q = qk[:, 0:GLA_HEADS * dk]
    kk = qk[:, GLA_HEADS * dk:2 * GLA_HEADS * dk]
    z = _dot(glow, wg2_ref[...]) + bg_ref[...]
    la = (jnp.minimum(z, 0.0) - jnp.log(1.0 + jnp.exp(-jnp.abs(z)))) * (1.0 / GLA_TAU)
    ri = lax.broadcasted_iota(jnp.int32, (rows, rows), 0)
    ci = lax.broadcasted_iota(jnp.int32, (rows, rows), 1)
    causal = (ri >= ci) & ((ri // GLA_CHUNK) == (ci // GLA_CHUNK))
    tri = jnp.where(causal, 1.0, 0.0).astype(BF16)
    la_hi = la.astype(BF16)
    la_lo = (la - la_hi.astype(F32)).astype(BF16)
    bcum = _dot(tri, la_hi) + _dot(tri, la_lo)
    n_chunks = rows // GLA_CHUNK
    blast = [bcum[(c + 1) * GLA_CHUNK - 1:(c + 1) * GLA_CHUNK, :] for c in range(n_chunks)]
    blast_full = jnp.concatenate(
        [jnp.broadcast_to(bl, (GLA_CHUNK, bl.shape[1])) for bl in blast], axis=0)
    q_dec = (q * (dk ** -0.5) * jnp.exp(bcum)).astype(BF16)
    k_intra = (kk * jnp.exp(-bcum)).astype(BF16)
    k_state = (kk * jnp.exp(blast_full - bcum)).astype(BF16)

    heads = []
    for h in range(GLA_HEADS):
        ks = slice(h * dk, (h + 1) * dk)
        v_h = v_ref[h]
        sc = lax.dot_general(q_dec[:, ks], k_intra[:, ks], (((1,), (1,)), ((), ())),
                             preferred_element_type=F32)
        o_h = _dot(jnp.where(causal, sc, 0.0).astype(BF16), v_h)
        st = s_ref[h]
        inter = []
        for c in range(n_chunks):
            rs = slice(c * GLA_CHUNK, (c + 1) * GLA_CHUNK)
            inter.append(lax.dot_general(q_dec[rs, ks], st.astype(BF16),
                                         (((1,), (1,)), ((), ())), preferred_element_type=F32))
            st = st * jnp.exp(blast[c][:, ks]) + lax.dot_general(
                v_h[rs, :], k_state[rs, ks], (((0,), (0,)), ((), ())),
                preferred_element_type=F32)
        s_ref[h] = st
        o_h = o_h + jnp.concatenate(inter, axis=0)
        o_h = _rms_scale(o_h, gn_ref[...])
        og_h = og_ref[h]
        heads.append((o_h * (og_h * _sigmoid(og_h))).astype(BF16))
    y_c = _dot(jnp.concatenate(heads, axis=1), gwo_ref[...])

    g_c = _sigmoid(_lane_concat(gc_ref) + gb_ref[:, 2 * d:3 * d])
    merged = merged_ab + g_c * y_c
    o_ref[...] = x + _dot(merged.astype(BF16), wout_ref[...])


_MIXER_PARAMS = [
    "mixn", "w_lru", "w_cm", "w_qk", "w_v", "w_og", "w_gl", "w_mgab", "w_mgc",
    "lcw", "lcb", "wa", "ba", "wx", "bx", "lam", "lwo",
    "ccw", "ccb", "lng", "lnb", "cwo",
    "wg2", "bg", "gn", "gwo", "gb", "wout",
]


def _col_chunks(w):
    k, n = w.shape
    return w.reshape(k, n // MXU_COLS, MXU_COLS).transpose(1, 0, 2)


def _mixer(x2, seq_len, weights):
    n, d = x2.shape
    assert seq_len % MIX_ROWS == 0 and MIX_ROWS % GLA_CHUNK == 0 and SEG % SUBLANES == 0
    tiles = seq_len // MIX_ROWS
    batch = n // seq_len
    full = lambda a: pl.BlockSpec(a.shape, lambda b, j: (0,) * a.ndim)
    row_spec = pl.BlockSpec((MIX_ROWS, d), lambda b, j: (b * tiles + j, 0))
    n_col = d // LANES
    assert list(weights) == _MIXER_PARAMS
    lru_taps = weights["lcw"].shape[0]
    cm_taps = weights["ccw"].shape[0]
    assert max(lru_taps, cm_taps) <= SEG
    dk = weights["wg2"].shape[1] // GLA_HEADS
    dv = weights["gwo"].shape[0] // GLA_HEADS
    chunks = d // MXU_COLS
    assert dv == MXU_COLS and 2 * GLA_HEADS * dk == chunks * MXU_COLS
    conv_scratch = lambda taps: [
        pltpu.VMEM((n_col, MIX_ROWS + SUBLANES * (taps - 1), LANES), F32),
        pltpu.VMEM((n_col, SUBLANES * (taps - 1), LANES), F32),
        pltpu.VMEM((n_col, SUBLANES * (taps + 1), LANES), F32),
        pltpu.VMEM((n_col, MIX_ROWS, LANES), F32),
    ]
    return pl.pallas_call(
        _mixer_body,
        grid=(batch, tiles),
        in_specs=[row_spec] + [full(w) for w in weights.values()],
        out_specs=row_spec,
        out_shape=jax.ShapeDtypeStruct((n, d), F32),
        scratch_shapes=[
            pltpu.VMEM((MIX_ROWS, MIX_ROWS), BF16),
            pltpu.VMEM((MIX_ROWS, d), BF16),
            pltpu.VMEM((MIX_ROWS, d), BF16),
            *conv_scratch(lru_taps),
            *conv_scratch(cm_taps),
            pltpu.VMEM((MIX_ROWS, d), F32),
            pltpu.VMEM((MIX_ROWS, d), F32),
            pltpu.VMEM((MIX_ROWS, d), BF16),
            pltpu.VMEM((n_col, MIX_ROWS, LANES), F32),
            pltpu.VMEM((chunks, MIX_ROWS, MXU_COLS), F32),
            pltpu.VMEM((chunks, MIX_ROWS, MXU_COLS), BF16),
            pltpu.VMEM((chunks, MIX_ROWS, MXU_COLS), F32),
            pltpu.VMEM((chunks, MIX_ROWS, MXU_COLS), F32),
            pltpu.VMEM((2 * chunks, MIX_ROWS, MXU_COLS), F32),
            pltpu.VMEM((chunks, MIX_ROWS, MXU_COLS), F32),
            pltpu.VMEM((SUBLANES, d), F32),
            pltpu.VMEM((GLA_HEADS, dv, dk), F32),
        ],
        compiler_params=pltpu.CompilerParams(
            dimension_semantics=("arbitrary", "arbitrary"), vmem_limit_bytes=VMEM_LIMIT_BYTES),
        name="mixer",
    )(x2, *weights.values())


def _row(v):
    return v.reshape(1, -1).astype(F32)


def kernel(x, ffn1_norm, ffn1_w_gate, ffn1_w_up, ffn1_w_down, mix_norm, w_in, lru_conv_w, lru_conv_b, lru_w_a, lru_b_a, lru_w_x, lru_b_x, lru_lambda, lru_w_o, cm_conv_w, cm_conv_b, cm_ln_g, cm_ln_b, cm_w_o, gla_w_g2, gla_b_g, gla_norm, gla_w_o, gate_b, w_out, ffn2_norm, ffn2_w_gate, ffn2_w_up, ffn2_w_down, final_norm):
    bsz, t, d = x.shape
    depth = w_in.shape[0]
    lru_w = lru_w_o.shape[1]
    cm_w = cm_w_o.shape[1]
    qk_w = gla_w_g2.shape[2]
    v_w = gla_w_o.shape[1]
    rank = gla_w_g2.shape[1]
    widths = (lru_w, 2 * cm_w, qk_w, qk_w, v_w, v_w, rank, gate_b.shape[1])
    offs = [0]
    for w in widths:
        offs.append(offs[-1] + w)
    assert offs[-1] == w_in.shape[2]
    rank_pad = -(-rank // LANES) * LANES

    x2 = x.reshape(bsz * t, d)
    fin = _row(final_norm)
    for i in range(depth):
        x2 = _ffn(x2, _row(ffn1_norm[i]), ffn1_w_gate[i].astype(BF16), ffn1_w_up[i].astype(BF16),
                  ffn1_w_down[i].astype(BF16), fin, final=False)
        wi = w_in[i].astype(BF16)
        seg = [wi[:, offs[s]:offs[s + 1]] for s in range(len(widths))]
        w_gl = jnp.pad(seg[6], ((0, 0), (0, rank_pad - rank)))
        w_g2 = jnp.pad(gla_w_g2[i].astype(BF16), ((0, rank_pad - rank), (0, 0)))
        weights = dict(
            mixn=_row(mix_norm[i]), w_lru=seg[0], w_cm=seg[1],
            w_qk=_col_chunks(wi[:, offs[2]:offs[4]]), w_v=_col_chunks(seg[4]),
            w_og=_col_chunks(seg[5]), w_gl=w_gl,
            w_mgab=_col_chunks(seg[7][:, 0:2 * d]), w_mgc=_col_chunks(seg[7][:, 2 * d:3 * d]),
            lcw=lru_conv_w[i], lcb=_row(lru_conv_b[i]), wa=lru_w_a[i].astype(BF16),
            ba=_row(lru_b_a[i]), wx=lru_w_x[i].astype(BF16), bx=_row(lru_b_x[i]),
            lam=_row(lru_lambda[i]), lwo=_col_chunks(lru_w_o[i].astype(BF16)),
            ccw=cm_conv_w[i], ccb=_row(cm_conv_b[i]), lng=_row(cm_ln_g[i]), lnb=_row(cm_ln_b[i]),
            cwo=cm_w_o[i].astype(BF16),
            wg2=w_g2, bg=_row(gla_b_g[i]), gn=_row(gla_norm[i]), gwo=gla_w_o[i].astype(BF16),
            gb=_row(gate_b[i]), wout=w_out[i].astype(BF16),
        )
        x2 = _mixer(x2, t, weights)
        x2 = _ffn(x2, _row(ffn2_norm[i]), ffn2_w_gate[i].astype(BF16), ffn2_w_up[i].astype(BF16),
                  ffn2_w_down[i].astype(BF16), fin, final=(i == depth - 1))
    return x2.reshape(bsz, t, d)
```
